```python
import math
import jax, jax.numpy as jnp
from jax import lax
import numpy as np

D_MODEL = 1024
BATCH = 16
SEQ = 2048
DEPTH = 4
DEC_BATCH = 8
DEC_SEQ = 8192
PAST_LEN = 128

HEAD_DIM = 64
A_HEADS = 8
A_KV_HEADS = 2
A_WINDOW = 128
A_BLOCK = 128
B_HEADS = 8
GRID_W = 64
NB_MAX_ROWS = 8
NB_COLS = 16
NB_COL_BLOCK = 16
NB_COL_SLAB = 32
ROPE_THETA = 10000.0
A_Q_W = A_HEADS * HEAD_DIM
A_KV_W = A_KV_HEADS * HEAD_DIM
B_W = B_HEADS * HEAD_DIM
ATTN_IN_W = A_Q_W + 2 * A_KV_W + 3 * B_W
ATTN_CAT_W = A_Q_W + B_W
C_HEADS = 8
C_KEY_DIM = D_MODEL // C_HEADS
C_WIDTH = C_HEADS * C_KEY_DIM
C_CHUNK = 64
N_GROUPS = 4
EXPERTS_PER_GROUP = 8
N_EXPERTS = N_GROUPS * EXPERTS_PER_GROUP
TOP_K_IN_GROUP = 2
D_EXPERT = D_MODEL // 2
MOE_BLOCK = 128
N_ATTN_LAYERS = (DEPTH + 1) // 2
N_REC_LAYERS = DEPTH // 2
DN_ALPHA = (2 * DEPTH) ** 0.25
DN_BETA = (8 * DEPTH) ** -0.25
LN_EPS = 1e-5
RMS_EPS = 1e-6

kernel_name = 'hybrid_bidir_encoder_swa_natten_hgrn2_hmoe'


def layer_norm(x, g, b):
    xf = x.astype(jnp.float32)
    mu = jnp.mean(xf, axis=-1, keepdims=True)
    var = jnp.mean(jnp.square(xf - mu), axis=-1, keepdims=True)
    return ((xf - mu) * lax.rsqrt(var + LN_EPS) * g.astype(jnp.float32) + b.astype(jnp.float32)).astype(x.dtype)


def rope(x, pos):
    inv = jnp.power(ROPE_THETA, -jnp.arange(0, HEAD_DIM, 2, dtype=jnp.float32) / HEAD_DIM)
    ang = pos.astype(jnp.float32)[:, None] * inv[None, :]
    cos = jnp.concatenate([jnp.cos(ang), jnp.cos(ang)], axis=-1)[None, :, None, :]
    sin = jnp.concatenate([jnp.sin(ang), jnp.sin(ang)], axis=-1)[None, :, None, :]
    x1, x2 = jnp.split(x, 2, axis=-1)
    rot = jnp.concatenate([-x2, x1], axis=-1)
    return (x * cos + rot * sin).astype(x.dtype)


def window_attention(q, k, v, sink):
    bsz, t = q.shape[0], q.shape[1]
    grp = A_HEADS // A_KV_HEADS
    n_blk = t // A_BLOCK
    span = A_BLOCK + 2 * A_WINDOW
    kp = jnp.pad(k, ((0, 0), (A_WINDOW, A_WINDOW), (0, 0), (0, 0)))
    vp = jnp.pad(v, ((0, 0), (A_WINDOW, A_WINDOW), (0, 0), (0, 0)))
    sink_l = sink.astype(jnp.float32).reshape(A_KV_HEADS, grp)
    scale = HEAD_DIM ** -0.5

    def block(j):
        q0 = j * A_BLOCK
        qb = lax.dynamic_slice_in_dim(q, q0, A_BLOCK, axis=1).reshape(bsz, A_BLOCK, A_KV_HEADS, grp, HEAD_DIM)
        kb = lax.dynamic_slice_in_dim(kp, q0, span, axis=1)
        vb = lax.dynamic_slice_in_dim(vp, q0, span, axis=1)
        qpos = q0 + jnp.arange(A_BLOCK)
        kpos = q0 - A_WINDOW + jnp.arange(span)
        valid = (jnp.abs(qpos[:, None] - kpos[None, :]) <= A_WINDOW) & (kpos >= 0)[None, :] & (kpos < t)[None, :]
        s = jnp.einsum('bqhgd,bkhd->bhgqk', qb, kb, preferred_element_type=jnp.float32) * scale
        s = jnp.where(valid, s, -jnp.inf)
        sk = jnp.broadcast_to(sink_l[None, :, :, None, None], (bsz, A_KV_HEADS, grp, A_BLOCK, 1))
        p = jax.nn.softmax(jnp.concatenate([s, sk], axis=-1), axis=-1)[..., :span]
        o = jnp.einsum('bhgqk,bkhd->bqhgd', p.astype(v.dtype), vb)
        return o.reshape(bsz, A_BLOCK, A_Q_W)

    out = lax.map(block, jnp.arange(n_blk))
    return out.transpose(1, 0, 2, 3).reshape(bsz, t, A_Q_W)


def neighbourhood_attention(q, k, v, rel_bias):
    bsz, t = q.shape[0], q.shape[1]
    rows = t // GRID_W
    wr = min(NB_MAX_ROWS, rows)
    ncb = GRID_W // NB_COL_BLOCK
    qg = q.reshape(bsz, rows, ncb, NB_COL_BLOCK, B_HEADS, HEAD_DIM)
    kg = k.reshape(bsz, rows, GRID_W, B_HEADS, HEAD_DIM)
    vg = v.reshape(bsz, rows, GRID_W, B_HEADS, HEAD_DIM)
    cb = np.arange(ncb)
    slab_start = np.clip(cb * NB_COL_BLOCK - NB_COLS // 2, 0, GRID_W - NB_COL_SLAB)
    kcol = slab_start[:, None] + np.arange(NB_COL_SLAB)
    qcol = cb[:, None] * NB_COL_BLOCK + np.arange(NB_COL_BLOCK)
    qwin = np.clip(qcol - NB_COLS // 2, 0, GRID_W - NB_COLS)
    col_ok = (kcol[:, None, :] >= qwin[:, :, None]) & (kcol[:, None, :] < qwin[:, :, None] + NB_COLS)
    ok = np.broadcast_to(col_ok[:, None, :, None, :], (ncb, 1, NB_COL_BLOCK, wr, NB_COL_SLAB)).reshape(ncb, 1, NB_COL_BLOCK, wr * NB_COL_SLAB)
    dc = np.clip(kcol[:, None, :] - qcol[:, :, None], -(NB_COLS - 1), NB_COLS - 1) + NB_COLS - 1
    col_bias = rel_bias.astype(jnp.float32)[:, :, dc]
    scale = HEAD_DIM ** -0.5

    def row(r):
        rs = jnp.clip(r - wr // 2, 0, rows - wr)
        qr = lax.dynamic_index_in_dim(qg, r, axis=1, keepdims=False)
        ks = lax.dynamic_slice_in_dim(kg, rs, wr, axis=1)
        vs = lax.dynamic_slice_in_dim(vg, rs, wr, axis=1)
        kb = ks[:, :, kcol].transpose(0, 2, 1, 3, 4, 5).reshape(bsz, ncb, wr * NB_COL_SLAB, B_HEADS, HEAD_DIM)
        vb = vs[:, :, kcol].transpose(0, 2, 1, 3, 4, 5).reshape(bsz, ncb, wr * NB_COL_SLAB, B_HEADS, HEAD_DIM)
        s = jnp.einsum('bjqhd,bjkhd->bjhqk', qr, kb, preferred_element_type=jnp.float32) * scale
        dr = rs + jnp.arange(wr) - r + NB_MAX_ROWS - 1
        bias = jnp.take(col_bias, dr, axis=1)
        bias = bias.transpose(2, 0, 3, 1, 4).reshape(ncb, B_HEADS, NB_COL_BLOCK, wr * NB_COL_SLAB)
        s = jnp.where(ok, s + bias, -jnp.inf)
        p = jax.nn.softmax(s, axis=-1)
        o = jnp.einsum('bjhqk,bjkhd->bjqhd', p.astype(v.dtype), vb)
        return o.reshape(bsz, GRID_W, B_W)

    out = lax.map(row, jnp.arange(rows))
    return out.transpose(1, 0, 2, 3).reshape(bsz, t, B_W)


def attention_mixer(h, w_in, sink, rel_bias, w_out):
    bsz, t, _ = h.shape
    proj = jnp.einsum('btd,de->bte', h, w_in)
    cuts = [A_Q_W, A_Q_W + A_KV_W, A_Q_W + 2 * A_KV_W, A_Q_W + 2 * A_KV_W + B_W, A_Q_W + 2 * A_KV_W + 2 * B_W]
    qa, ka, va, qb, kb, vb = jnp.split(proj, cuts, axis=-1)
    pos = jnp.arange(t)
    qa = rope(qa.reshape(bsz, t, A_HEADS, HEAD_DIM), pos)
    ka = rope(ka.reshape(bsz, t, A_KV_HEADS, HEAD_DIM), pos)
    va = va.reshape(bsz, t, A_KV_HEADS, HEAD_DIM)
    oa = window_attention(qa, ka, va, sink)
    ob = neighbourhood_attention(qb.reshape(bsz, t, B_HEADS, HEAD_DIM), kb.reshape(bsz, t, B_HEADS, HEAD_DIM), vb.reshape(bsz, t, B_HEADS, HEAD_DIM), rel_bias)
    return jnp.einsum('bte,ed->btd', jnp.concatenate([oa, ob], axis=-1), w_out)


def hgrn_direction(q, z, v, lb):
    bsz, nh, t, dk = q.shape
    dv = v.shape[-1]
    nc = t // C_CHUNK
    lbb = lb[None, :, None, :]
    f = lbb + (1.0 - lbb) * jax.nn.sigmoid(z)
    logf = jnp.log(f)
    k = 1.0 - f

    def chunks(a):
        return a.reshape(bsz, nh, nc, C_CHUNK, a.shape[-1]).transpose(2, 0, 1, 3, 4)

    qc, kc, vc = chunks(q), chunks(k), chunks(v)
    bc = jnp.cumsum(chunks(logf), axis=3)
    incl = np.tril(np.ones((C_CHUNK, C_CHUNK), dtype=bool))

    def step(s_state, xs):
        qi, ki, vi, bi = xs
        diff = bi[:, :, :, None, :] - bi[:, :, None, :, :]
        dec = jnp.exp(jnp.where(incl[:, :, None], diff, -jnp.inf))
        a = jnp.einsum('bhtk,bhsk,bhtsk->bhts', qi, ki, dec)
        bl = bi[:, :, -1, :]
        o = jnp.einsum('bhts,bhsv->bhtv', a, vi) + jnp.einsum('bhtk,bhkv->bhtv', qi * jnp.exp(bi), s_state)
        s_state = jnp.exp(bl)[..., None] * s_state + jnp.einsum('bhsk,bhsv->bhkv', ki * jnp.exp(bl[:, :, None, :] - bi), vi)
        return s_state, o

    s0 = jnp.zeros((bsz, nh, dk, dv), jnp.float32)
    _, o = lax.scan(step, s0, (qc, kc, vc, bc))
    return o.transpose(1, 2, 0, 3, 4).reshape(bsz, nh, t, dv)


def hgrn_mixer(h, w_in, lb, gnorm, w_out):
    bsz, t, _ = h.shape
    proj = jnp.einsum('btd,de->bte', h, w_in).astype(jnp.float32)
    q, zf, zb, v, g = jnp.split(proj, 5, axis=-1)

    def heads(a):
        return a.reshape(bsz, t, C_HEADS, C_KEY_DIM).transpose(0, 2, 1, 3)

    q, zf, zb, v = heads(q), heads(zf), heads(zb), heads(v)
    lbh = lb.reshape(C_HEADS, C_KEY_DIM)
    o_f = hgrn_direction(q, zf, v, lbh)
    o_b = jnp.flip(hgrn_direction(jnp.flip(q, axis=2), jnp.flip(zb, axis=2), jnp.flip(v, axis=2), lbh), axis=2)
    o = (o_f + o_b).transpose(0, 2, 1, 3)
    o = o * lax.rsqrt(jnp.mean(jnp.square(o), axis=-1, keepdims=True) + RMS_EPS)
    o = o.reshape(bsz, t, C_WIDTH) * gnorm.astype(jnp.float32) * jax.nn.silu(g)
    return jnp.einsum('bte,ed->btd', o.astype(h.dtype), w_out)


def hierarchical_moe(h, w_rg, b_rg, w_re, b_re, w_gate, w_up, w_down):
    n_tok, d = h.shape
    hf = h.astype(jnp.float32)
    g_prob = jax.nn.softmax(hf @ w_rg.astype(jnp.float32) + b_rg.astype(jnp.float32), axis=-1)
    p_grp, grp = lax.top_k(g_prob, 1)
    e_logits = (hf @ w_re.astype(jnp.float32)).reshape(n_tok, N_GROUPS, EXPERTS_PER_GROUP) + b_re.astype(jnp.float32)
    e_logits = jnp.take_along_axis(e_logits, grp[:, :, None], axis=1)[:, 0]
    top_val, top_idx = lax.top_k(e_logits, TOP_K_IN_GROUP)
    gate = jax.nn.softmax(top_val, axis=-1) * p_grp
    n_asg = n_tok * TOP_K_IN_GROUP
    eid = (grp * EXPERTS_PER_GROUP + top_idx).reshape(n_asg)
    tok = jnp.repeat(jnp.arange(n_tok, dtype=jnp.int32), TOP_K_IN_GROUP)
    wt = gate.reshape(n_asg)
    order = jnp.argsort(eid)
    eid_s, tok_s, wt_s = eid[order], tok[order], wt[order]
    counts = jax.ops.segment_sum(jnp.ones_like(eid_s), eid_s, num_segments=N_EXPERTS)
    starts = jnp.cumsum(counts) - counts
    padded = (counts + MOE_BLOCK - 1) // MOE_BLOCK * MOE_BLOCK
    pad_ends = jnp.cumsum(padded)
    pad_starts = pad_ends - padded
    rank = jnp.arange(n_asg, dtype=jnp.int32) - starts[eid_s]
    dest = pad_starts[eid_s] + rank
    cap = -(-n_asg // MOE_BLOCK) * MOE_BLOCK + N_EXPERTS * MOE_BLOCK
    n_blk = cap // MOE_BLOCK
    xbuf = jnp.zeros((cap, d), h.dtype).at[dest].set(h[tok_s])
    blk_exp = jnp.minimum(jnp.searchsorted(pad_ends, jnp.arange(n_blk, dtype=jnp.int32) * MOE_BLOCK, side='right'), N_EXPERTS - 1)

    def run_block(args):
        xb, e = args
        return (jax.nn.silu(xb @ w_gate[e]) * (xb @ w_up[e])) @ w_down[e]

    ybuf = lax.map(run_block, (xbuf.reshape(n_blk, MOE_BLOCK, d), blk_exp)).reshape(cap, d)
    y = jnp.zeros((n_tok, d), jnp.float32).at[tok_s].add(ybuf[dest].astype(jnp.float32) * wt_s[:, None])
    return y.astype(h.dtype)


def encoder_trunk(x, c, ada_w, ada_b, ln_g, ln_b, attn_w_in, attn_sink, nat_rel_bias, attn_w_out, rec_w_in, rec_lb, rec_gnorm, rec_w_out, router_w_group, router_b_group, router_w_expert, router_b_expert, expert_w_gate, expert_w_up, expert_w_down):
    bsz, t, d = x.shape
    p_lb = jax.nn.softmax(rec_lb.astype(jnp.float32), axis=0)
    lower_bounds = jnp.cumsum(p_lb, axis=0) - p_lb[0:1]
    cs = jax.nn.silu(c)
    for l in range(DEPTH):
        mod = cs @ ada_w[l] + ada_b[l]
        sh1, sc1, g1, sh2, sc2, g2 = [m[:, None, :] for m in jnp.split(mod, 6, axis=-1)]
        hmix = x * (1 + sc1) + sh1
        if l % 2 == 0:
            i = l // 2
            y = attention_mixer(hmix, attn_w_in[i], attn_sink[i], nat_rel_bias[i], attn_w_out[i])
        else:
            i = l // 2
            y = hgrn_mixer(hmix, rec_w_in[i], lower_bounds[i], rec_gnorm[i], rec_w_out[i])
        x = layer_norm(DN_ALPHA * x + (1 + g1) * y, ln_g[l, 0], ln_b[l, 0])
        hffn = x * (1 + sc2) + sh2
        y = hierarchical_moe(hffn.reshape(bsz * t, d), router_w_group[l], router_b_group[l], router_w_expert[l], router_b_expert[l], expert_w_gate[l], expert_w_up[l], expert_w_down[l]).reshape(bsz, t, d)
        x = layer_norm(DN_ALPHA * x + (1 + g2) * y, ln_g[l, 1], ln_b[l, 1])
    return x


def setup_inputs(seed: int = 0) -> dict:
    key = jax.random.key(seed)
    ks = jax.random.split(key, 24)
    f32 = jnp.float32
    nrm = lambda k, s: jax.random.normal(k, s, f32)
    d = D_MODEL
    return {
        'x_prompt': nrm(ks[0], (BATCH, SEQ, d)),
        'x_sample': nrm(ks[1], (DEC_BATCH, DEC_SEQ, d)),
        'c_prompt': nrm(ks[2], (BATCH, d)),
        'c_sample': nrm(ks[3], (DEC_BATCH, d)),
        'ada_w': nrm(ks[4], (DEPTH, d, 6 * d)) * (0.2 * d ** -0.5),
        'ada_b': nrm(ks[5], (DEPTH, 6 * d)) * 0.02,
        'ln_g': 1.0 + 0.02 * nrm(ks[6], (DEPTH, 2, d)),
        'ln_b': 0.02 * nrm(ks[7], (DEPTH, 2, d)),
        'attn_w_in': nrm(ks[8], (N_ATTN_LAYERS, d, ATTN_IN_W)) * d ** -0.5,
        'attn_sink': nrm(ks[9], (N_ATTN_LAYERS, A_HEADS)) * 0.5,
        'nat_rel_bias': nrm(ks[10], (N_ATTN_LAYERS, B_HEADS, 2 * NB_MAX_ROWS - 1, 2 * NB_COLS - 1)) * 0.1,
        'attn_w_out': nrm(ks[11], (N_ATTN_LAYERS, ATTN_CAT_W, d)) * (ATTN_CAT_W ** -0.5 * DN_BETA),
        'rec_w_in': nrm(ks[12], (N_REC_LAYERS, d, 5 * C_WIDTH)) * d ** -0.5,
        'rec_lb': nrm(ks[13], (N_REC_LAYERS, C_WIDTH)) * 0.5,
        'rec_gnorm': 1.0 + 0.02 * nrm(ks[14], (N_REC_LAYERS, C_WIDTH)),
        'rec_w_out': nrm(ks[15], (N_REC_LAYERS, C_WIDTH, d)) * (C_WIDTH ** -0.5 * DN_BETA),
        'router_w_group': nrm(ks[16], (DEPTH, d, N_GROUPS)) * d ** -0.5,
        'router_b_group': nrm(ks[17], (DEPTH, N_GROUPS)) * 0.01,
        'router_w_expert': nrm(ks[18], (DEPTH, d, N_EXPERTS)) * d ** -0.5,
        'router_b_expert': nrm(ks[19], (DEPTH, N_GROUPS, EXPERTS_PER_GROUP)) * 0.01,
        'expert_w_gate': nrm(ks[20], (DEPTH, N_EXPERTS, d, D_EXPERT)) * d ** -0.5,
        'expert_w_up': nrm(ks[21], (DEPTH, N_EXPERTS, d, D_EXPERT)) * d ** -0.5,
        'expert_w_down': nrm(ks[22], (DEPTH, N_EXPERTS, D_EXPERT, d)) * (D_EXPERT ** -0.5 * DN_BETA),
    }


def reference(x_prompt, x_sample, c_prompt, c_sample, ada_w, ada_b, ln_g, ln_b, attn_w_in, attn_sink, nat_rel_bias, attn_w_out, rec_w_in, rec_lb, rec_gnorm, rec_w_out, router_w_group, router_b_group, router_w_expert, router_b_expert, expert_w_gate, expert_w_up, expert_w_down):
    y_prompt = encoder_trunk(x_prompt, c_prompt, ada_w, ada_b, ln_g, ln_b, attn_w_in, attn_sink, nat_rel_bias, attn_w_out, rec_w_in, rec_lb, rec_gnorm, rec_w_out, router_w_group, router_b_group, router_w_expert, router_b_expert, expert_w_gate, expert_w_up, expert_w_down)
    y_sample = encoder_trunk(x_sample, c_sample, ada_w, ada_b, ln_g, ln_b, attn_w_in, attn_sink, nat_rel_bias, attn_w_out, rec_w_in, rec_lb, rec_gnorm, rec_w_out, router_w_group, router_b_group, router_w_expert, router_b_expert, expert_w_gate, expert_w_up, expert_w_down)
    return (y_prompt, y_sample)
```

```python
import functools

import numpy as np
import jax
import jax.numpy as jnp
from jax import lax
from jax.experimental import pallas as pl
from jax.experimental.pallas import tpu as pltpu

F32 = jnp.float32
BF16 = jnp.bfloat16
NEG_INF = float("-inf")

D_MODEL = 1024
HEAD_DIM = 64
A_HEADS = 8
A_KV_HEADS = 2
A_WINDOW = 128
A_BLOCK = 128
B_HEADS = 8
GRID_W = 64
NB_ROWS = 8
NB_COLS = 16
ROPE_THETA = 10000.0
C_HEADS = 8
C_KEY_DIM = 128
C_CHUNK = 64
N_GROUPS = 4
EXPERTS_PER_GROUP = 8
N_EXPERTS = 32
D_EXPERT = 512
LN_EPS = 1e-5
RMS_EPS = 1e-6

LANES = 128
ATTN_W = 2560
ROPE_W = 768
REC_W = 5 * D_MODEL
ROUTER_ROWS = 40
ROUTE_ROWS = 16
GATE_W = LANES
XBUF_W = D_MODEL + GATE_W

TM = 512
MOE_BLK = 256
NAT_BLK = 512
REC_BLK = 512
VMEM_LIMIT = 48 * 1024 * 1024


def _cparams(sem):
    return pltpu.CompilerParams(dimension_semantics=sem, vmem_limit_bytes=VMEM_LIMIT)


def _decode(i, blk, segs):
    out = None
    blk0 = 0
    for (b, t) in segs:
        nb = t // blk
        j = (i - blk0) % nb
        cand = (i - j, j, nb)
        out = cand if out is None else tuple(jnp.where(i < blk0, o, c) for o, c in zip(out, cand))
        blk0 += b * nb
    return out


def _mod_row(i, blk, segs):
    out = None
    blk0 = 0
    row0 = 0
    for (b, t) in segs:
        nb = t // blk
        cand = row0 + (i - blk0) // nb
        out = cand if out is None else jnp.where(i < blk0, out, cand)
        blk0 += b * nb
        row0 += b
    return out


def _pos_block(i, blk, segs):
    return _decode(i, blk, segs)[1]


def _dot(a, b):
    return jnp.dot(a, b, preferred_element_type=F32)


def _dot_nt(a, b):
    return lax.dot_general(a, b, (((1,), (1,)), ((), ())), preferred_element_type=F32)


def _dot_tn(a, b):
    return lax.dot_general(a, b, (((0,), (0,)), ((), ())), preferred_element_type=F32)


def _split2(x):
    hi = x.astype(BF16)
    lo = (x - hi.astype(F32)).astype(BF16)
    return hi, lo


def _layer_norm(x, g, b):
    mu = jnp.mean(x, axis=-1, keepdims=True)
    xc = x - mu
    var = jnp.mean(xc * xc, axis=-1, keepdims=True)
    return xc * lax.rsqrt(var + LN_EPS) * g + b


def _adaln_kernel(c_ref, w_ref, b_ref, o_ref):
    c = c_ref[...]
    cs = c * jax.nn.sigmoid(c)
    c_hi, c_lo = _split2(cs)
    w_hi, w_lo = _split2(w_ref[0])
    o_ref[0] = _dot(c_hi, w_hi) + _dot(c_hi, w_lo) + _dot(c_lo, w_hi) + b_ref[0]


def _adaln(c, ada_w, ada_b):
    depth, d, n = ada_w.shape
    nb = c.shape[0]
    tn = 1536
    return pl.pallas_call(
        _adaln_kernel,
        grid=(depth, n // tn),
        in_specs=[
            pl.BlockSpec((nb, d), lambda l, j: (0, 0)),
            pl.BlockSpec((1, d, tn), lambda l, j: (l, 0, j)),
            pl.BlockSpec((1, 1, tn), lambda l, j: (l, 0, j)),
        ],
        out_specs=pl.BlockSpec((1, nb, tn), lambda l, j: (l, 0, j)),
        out_shape=jax.ShapeDtypeStruct((depth, nb, n), F32),
        compiler_params=_cparams(("parallel", "parallel")),
        name="adaln",
    )(c, ada_w, ada_b.reshape(depth, 1, n))


def _rot_half_pairs(x):
    lane = lax.broadcasted_iota(jnp.int32, x.shape, 1)
    first = (lane & 63) < 32
    return jnp.where(first, pltpu.roll(x, 96, 1), pltpu.roll(x, 32, 1))


def _inproj_kernel(x_ref, mod_ref, w_ref, *rest, rope):
    if rope:
        cos_ref, sin_ref, o_ref, xb_ref = rest
    else:
        o_ref, xb_ref = rest
    j = pl.program_id(1)

    @pl.when(j == 0)
    def _():
        m = mod_ref[0]
        xb_ref[...] = (x_ref[...] * (1.0 + m[1:2]) + m[0:1]).astype(BF16)

    y = _dot(xb_ref[...], w_ref[...])
    if not rope:
        o_ref[...] = y.astype(o_ref.dtype)
        return

    @pl.when(j == 0)
    def _():
        cos = cos_ref[...]
        sin = sin_ref[...]
        for g in range(ROPE_W // LANES):
            blk = y[:, g * LANES:(g + 1) * LANES]
            o_ref[:, g * LANES:(g + 1) * LANES] = (blk * cos + _rot_half_pairs(blk) * sin).astype(o_ref.dtype)
        o_ref[:, ROPE_W:] = y[:, ROPE_W:].astype(o_ref.dtype)

    @pl.when(j != 0)
    def _():
        o_ref[...] = y.astype(o_ref.dtype)


def _inproj(x, mod_l, w, segs, *, tn, out_dtype, rope_tables=None):
    nt, d = x.shape
    n = w.shape[1]
    rope = rope_tables is not None
    in_specs = [
        pl.BlockSpec((TM, d), lambda i, j: (i, 0)),
        pl.BlockSpec((1, 6, d), lambda i, j: (_mod_row(i, TM, segs), 0, 0)),
        pl.BlockSpec((d, tn), lambda i, j: (0, j)),
    ]
    args = [x, mod_l, w]
    if rope:
        in_specs += [pl.BlockSpec((TM, LANES), lambda i, j: (_pos_block(i, TM, segs), 0))] * 2
        args += list(rope_tables)
    return pl.pallas_call(
        functools.partial(_inproj_kernel, rope=rope),
        grid=(nt // TM, n // tn),
        in_specs=in_specs,
        out_specs=pl.BlockSpec((TM, tn), lambda i, j: (i, j)),
        out_shape=jax.ShapeDtypeStruct((nt, n), out_dtype),
        scratch_shapes=[pltpu.VMEM((TM, d), BF16)],
        compiler_params=_cparams(("parallel", "arbitrary")),
        name="inproj_rope" if rope else "inproj",
    )(*args)


def _win_attn_kernel(sink_ref, q_ref, k0_ref, k1_ref, k2_ref, v0_ref, v1_ref, v2_ref, o_ref, *, segs):
    i = pl.program_id(0)
    _, j, nb = _decode(i, A_BLOCK, segs)
    ws = jnp.clip(j - 1, 0, nb - 3)
    span = 3 * A_BLOCK
    qpos = j * A_BLOCK + lax.broadcasted_iota(jnp.int32, (A_BLOCK, span), 0)
    kpos = ws * A_BLOCK + lax.broadcasted_iota(jnp.int32, (A_BLOCK, span), 1)
    valid = jnp.abs(qpos - kpos) <= A_WINDOW
    lo = lax.broadcasted_iota(jnp.int32, (A_BLOCK, LANES), 1) < HEAD_DIM
    kfull = jnp.concatenate([k0_ref[...], k1_ref[...], k2_ref[...]], axis=0)
    vfull = jnp.concatenate([v0_ref[...], v1_ref[...], v2_ref[...]], axis=0)
    zero = jnp.zeros((A_BLOCK, LANES), BF16)
    grp = A_HEADS // A_KV_HEADS
    for kvh in range(A_KV_HEADS):
        kk = kfull[:, kvh * LANES:(kvh + 1) * LANES]
        vv = vfull[:, kvh * LANES:(kvh + 1) * LANES]
        parts = []
        for p in range(grp // 2):
            c0 = (kvh * (grp // 2) + p) * LANES
            qp = q_ref[:, c0:c0 + LANES]
            parts.append(jnp.where(lo, qp, zero))
            parts.append(jnp.where(lo, zero, qp))
        qs = jnp.concatenate(parts, axis=0)
        s = _dot_nt(qs, kk)
        probs = []
        inv = []
        for h in range(grp):
            sh = jnp.where(valid, s[h * A_BLOCK:(h + 1) * A_BLOCK], NEG_INF)
            sink = sink_ref[kvh * grp + h]
            m = jnp.maximum(jnp.max(sh, axis=-1, keepdims=True), sink)
            e = jnp.exp(sh - m)
            den = jnp.sum(e, axis=-1, keepdims=True) + jnp.exp(sink - m)
            probs.append(e.astype(BF16))
            inv.append(1.0 / den)
        o = _dot(jnp.concatenate(probs, axis=0), vv)
        for p in range(grp // 2):
            oe = o[(2 * p) * A_BLOCK:(2 * p + 1) * A_BLOCK] * inv[2 * p]
            oo = o[(2 * p + 1) * A_BLOCK:(2 * p + 2) * A_BLOCK] * inv[2 * p + 1]
            c0 = (kvh * (grp // 2) + p) * LANES
            o_ref[:, c0:c0 + LANES] = jnp.where(lo, oe, oo).astype(o_ref.dtype)


def _win_attn(proj, sink, segs):
    nt = proj.shape[0]
    kvw = 2 * LANES

    def kv_map(d, col):
        def f(i):
            base, j, nb = _decode(i, A_BLOCK, segs)
            return (base + jnp.clip(j - 1, 0, nb - 3) + d, col)
        return f

    in_specs = [pl.BlockSpec(memory_space=pltpu.SMEM),
                pl.BlockSpec((A_BLOCK, 512), lambda i: (i, 0))]
    in_specs += [pl.BlockSpec((A_BLOCK, kvw), kv_map(d, 2)) for d in range(3)]
    in_specs += [pl.BlockSpec((A_BLOCK, kvw), kv_map(d, 3)) for d in range(3)]
    return pl.pallas_call(
        functools.partial(_win_attn_kernel, segs=segs),
        grid=(nt // A_BLOCK,),
        in_specs=in_specs,
        out_specs=pl.BlockSpec((A_BLOCK, 512), lambda i: (i, 0)),
        out_shape=jax.ShapeDtypeStruct((nt, 512), BF16),
        compiler_params=_cparams(("parallel",)),
        name="win_attn",
    )(sink, proj, proj, proj, proj, proj, proj, proj)


def _nat_bias_table(rel_bias):
    c = np.arange(GRID_W)
    kc = np.arange(GRID_W)
    qwin = np.clip(c - NB_COLS // 2, 0, GRID_W - NB_COLS)
    ok = (kc[None, :] >= qwin[:, None]) & (kc[None, :] < qwin[:, None] + NB_COLS)
    dc = np.clip(kc[None, :] - c[:, None], -(NB_COLS - 1), NB_COLS - 1) + NB_COLS - 1
    var = np.arange(NB_ROWS)
    irow = np.arange(NB_ROWS)
    dr = irow[None, :] - var[:, None] + NB_ROWS - 1
    t = rel_bias.astype(F32)[:, dr][:, :, :, dc]
    t = jnp.where(jnp.asarray(ok)[None, None, None], t, NEG_INF)
    t = t.transpose(1, 0, 3, 2, 4)
    return t.reshape(NB_ROWS, B_HEADS, GRID_W, NB_ROWS * GRID_W)


def _nat_attn_kernel(q_ref, k0, k1, k2, k3, v0, v1, v2, v3, bias_ref, o_ref, kbuf, vbuf, *, segs):
    i = pl.program_id(0)
    _, j, nb = _decode(i, NAT_BLK, segs)
    rows = nb * NB_ROWS
    w0 = jnp.clip(NB_ROWS * j - NB_ROWS // 2, 0, rows - 2 * NB_ROWS)
    for d, (kr, vr) in enumerate(((k0, v0), (k1, v1), (k2, v2), (k3, v3))):
        kbuf[d * 256:(d + 1) * 256, :] = kr[...]
        vbuf[d * 256:(d + 1) * 256, :] = vr[...]
    lo = lax.broadcasted_iota(jnp.int32, (GRID_W, LANES), 1) < HEAD_DIM
    zero = jnp.zeros((GRID_W, LANES), BF16)
    nkeys = NB_ROWS * GRID_W
    for rr in range(NB_ROWS):
        r = NB_ROWS * j + rr
        rs = jnp.clip(r - NB_ROWS // 2, 0, rows - NB_ROWS)
        var = r - rs
        off = pl.multiple_of((rs - w0) * GRID_W, GRID_W)
        for hp in range(B_HEADS // 2):
            cs = slice(hp * LANES, (hp + 1) * LANES)
            kp = kbuf[pl.ds(off, nkeys), cs]
            vp = vbuf[pl.ds(off, nkeys), cs]
            qp = q_ref[rr * GRID_W:(rr + 1) * GRID_W, cs]
            qs = jnp.concatenate([jnp.where(lo, qp, zero), jnp.where(lo, zero, qp)], axis=0)
            s = _dot_nt(qs, kp)
            bias = jnp.concatenate([bias_ref[var, 2 * hp], bias_ref[var, 2 * hp + 1]], axis=0)
            s = s + bias
            m = jnp.max(s, axis=-1, keepdims=True)
            e = jnp.exp(s - m)
            inv = 1.0 / jnp.sum(e, axis=-1, keepdims=True)
            o = _dot(e.astype(BF16), vp) * inv
            o_ref[rr * GRID_W:(rr + 1) * GRID_W, cs] = jnp.where(lo, o[:GRID_W], o[GRID_W:]).astype(o_ref.dtype)


def _nat_attn(proj, bias_table, segs):
    nt = proj.shape[0]

    def kv_map(d, col):
        def f(i):
            base, j, nb = _decode(i, NAT_BLK, segs)
            return (2 * base + jnp.clip(2 * j - 1, 0, 2 * nb - 4) + d, col)
        return f

    in_specs = [pl.BlockSpec((NAT_BLK, 512), lambda i: (i, 2))]
    in_specs += [pl.BlockSpec((256, 512), kv_map(d, 3)) for d in range(4)]
    in_specs += [pl.BlockSpec((256, 512), kv_map(d, 4)) for d in range(4)]
    in_specs += [pl.BlockSpec(bias_table.shape, lambda i: (0, 0, 0, 0))]
    return pl.pallas_call(
        functools.partial(_nat_attn_kernel, segs=segs),
        grid=(nt // NAT_BLK,),
        in_specs=in_specs,
        out_specs=pl.BlockSpec((NAT_BLK, 512), lambda i: (i, 0)),
        out_shape=jax.ShapeDtypeStruct((nt, 512), BF16),
        scratch_shapes=[pltpu.VMEM((1024, 512), BF16), pltpu.VMEM((1024, 512), BF16)],
        compiler_params=_cparams(("parallel",)),
        name="nat_attn",
    )(*([proj] * 9), bias_table)


def _boundary_rows(b, m, fwd):
    n = b.shape[0]
    pick = m - 1 if fwd else m
    if 2 * m >= 8:
        b3 = b.reshape(n // (2 * m), 2 * m, LANES)
        return jnp.broadcast_to(b3[:, pick:pick + 1, :], b3.shape).reshape(n, LANES)
    b3 = b.reshape(n // 8, 8, LANES)
    sub = lax.broadcasted_iota(jnp.int32, b3.shape, 1)
    r = None
    for g in reversed(range(8 // (2 * m))):
        row = g * 2 * m + pick
        bc = jnp.broadcast_to(b3[:, row:row + 1, :], b3.shape)
        r = bc if r is None else jnp.where(sub < (g + 1) * 2 * m, bc, r)
    return r.reshape(n, LANES)


def _hgrn_chunk(q, z, v, lower, tri, st_ref, fwd):
    n = C_CHUNK
    f = lower + (1.0 - lower) * jax.nn.sigmoid(z)
    logf = jnp.log(f)
    kk = 1.0 - f
    l_hi, l_lo = _split2(logf)
    b = _dot(tri, l_hi) + _dot(tri, l_lo)
    total = b[n - 1:n] if fwd else b[0:1]
    rowi = lax.broadcasted_iota(jnp.int32, (n, 1), 0)
    ti = lax.broadcasted_iota(jnp.int32, (n, n), 0)
    si = lax.broadcasted_iota(jnp.int32, (n, n), 1)
    a = jnp.where(ti == si, jnp.sum(q * kk, axis=-1, keepdims=True), 0.0)
    m = n // 2
    while m >= 1:
        e = jnp.exp(-jnp.abs(b - _boundary_rows(b, m, fwd)))
        upper = (rowi & (2 * m - 1)) >= m
        qmask = upper if fwd else jnp.logical_not(upper)
        qm = jnp.where(qmask, q * e, 0.0).astype(BF16)
        km = jnp.where(qmask, 0.0, kk * e).astype(BF16)
        shift = int(np.log2(2 * m))
        same = (ti >> shift) == (si >> shift)
        a = a + jnp.where(same, _dot_nt(qm, km), 0.0)
        m //= 2
    st = st_ref[...]
    o = _dot(a.astype(BF16), v.astype(BF16))
    o = o + _dot_nt((q * jnp.exp(b)).astype(BF16), st.astype(BF16))
    ke = (kk * jnp.exp(total - b)).astype(BF16)
    st_ref[...] = st * jnp.exp(total) + _dot_tn(v.astype(BF16), ke)
    return o


def _hgrn_kernel(lb_ref, trif_ref, trib_ref, qf_ref, zf_ref, vf_ref, qb_ref, zb_ref, vb_ref,
                 of_ref, ob_ref, sf_ref, sb_ref, *, segs, layer):
    i = pl.program_id(1)
    _, j, _ = _decode(i, REC_BLK, segs)

    @pl.when(j == 0)
    def _():
        sf_ref[...] = jnp.zeros_like(sf_ref)
        sb_ref[...] = jnp.zeros_like(sb_ref)

    lb = lb_ref[...]
    ex = jnp.exp(lb - jnp.max(lb, axis=0, keepdims=True))
    p = ex / jnp.sum(ex, axis=0, keepdims=True)
    lower = jnp.sum(p[0:layer + 1], axis=0, keepdims=True) - p[0:1]
    trif = trif_ref[...]
    trib = trib_ref[...]
    nch = REC_BLK // C_CHUNK

    def body(c, carry):
        of = pl.multiple_of(c * C_CHUNK, C_CHUNK)
        o = _hgrn_chunk(qf_ref[pl.ds(of, C_CHUNK), :], zf_ref[pl.ds(of, C_CHUNK), :],
                        vf_ref[pl.ds(of, C_CHUNK), :], lower, trif, sf_ref, True)
        of_ref[pl.ds(of, C_CHUNK), :] = o
        ob = pl.multiple_of((nch - 1 - c) * C_CHUNK, C_CHUNK)
        o = _hgrn_chunk(qb_ref[pl.ds(ob, C_CHUNK), :], zb_ref[pl.ds(ob, C_CHUNK), :],
                        vb_ref[pl.ds(ob, C_CHUNK), :], lower, trib, sb_ref, False)
        ob_ref[pl.ds(ob, C_CHUNK), :] = o
        return carry

    lax.fori_loop(0, nch, body, 0)


def _hgrn(proj, rec_lb, layer, segs):
    nt = proj.shape[0]
    nrec = rec_lb.shape[0]
    idx = np.arange(C_CHUNK)
    trif = jnp.asarray(idx[:, None] >= idx[None, :], BF16)
    trib = jnp.asarray(idx[:, None] <= idx[None, :], BF16)

    def fwd_map(col):
        return lambda h, i: (i, col * C_HEADS + h)

    def bwd_map(col):
        def f(h, i):
            base, j, nb = _decode(i, REC_BLK, segs)
            return (base + nb - 1 - j, col * C_HEADS + h)
        return f

    blk = (REC_BLK, C_KEY_DIM)
    in_specs = [pl.BlockSpec((nrec, C_KEY_DIM), lambda h, i: (0, h)),
                pl.BlockSpec((C_CHUNK, C_CHUNK), lambda h, i: (0, 0)),
                pl.BlockSpec((C_CHUNK, C_CHUNK), lambda h, i: (0, 0)),
                pl.BlockSpec(blk, fwd_map(0)), pl.BlockSpec(blk, fwd_map(1)), pl.BlockSpec(blk, fwd_map(3)),
                pl.BlockSpec(blk, bwd_map(0)), pl.BlockSpec(blk, bwd_map(2)), pl.BlockSpec(blk, bwd_map(3))]
    out_shape = [jax.ShapeDtypeStruct((nt, D_MODEL), F32)] * 2
    return pl.pallas_call(
        functools.partial(_hgrn_kernel, segs=segs, layer=layer),
        grid=(C_HEADS, nt // REC_BLK),
        in_specs=in_specs,
        out_specs=[pl.BlockSpec(blk, fwd_map(0)), pl.BlockSpec(blk, bwd_map(0))],
        out_shape=out_shape,
        scratch_shapes=[pltpu.VMEM((C_KEY_DIM, C_KEY_DIM), F32)] * 2,
        compiler_params=_cparams(("parallel", "arbitrary")),
        name="hgrn",
    )(rec_lb, trif, trib, proj, proj, proj, proj, proj, proj)


def _route_rows(logits_t, utri, cnt_ref):
    tm = logits_t.shape[1]
    gi = lax.broadcasted_iota(jnp.int32, (8, tm), 0).astype(F32)
    lg = jnp.where(gi < N_GROUPS, logits_t[0:8], NEG_INF)
    gm = jnp.max(lg, axis=0, keepdims=True)
    p_grp = 1.0 / jnp.sum(jnp.exp(lg - gm), axis=0, keepdims=True)
    grp = jnp.min(jnp.where(lg == gm, gi, 99.0), axis=0, keepdims=True)
    ei = lax.broadcasted_iota(jnp.int32, (N_EXPERTS, tm), 0).astype(F32)
    egrp = (lax.broadcasted_iota(jnp.int32, (N_EXPERTS, tm), 0) >> 3).astype(F32)
    le = jnp.where(egrp == grp, logits_t[8:8 + N_EXPERTS], NEG_INF)
    v1 = jnp.max(le, axis=0, keepdims=True)
    i1 = jnp.min(jnp.where(le == v1, ei, 99.0), axis=0, keepdims=True)
    le2 = jnp.where(ei == i1, NEG_INF, le)
    v2 = jnp.max(le2, axis=0, keepdims=True)
    i2 = jnp.min(jnp.where(le2 == v2, ei, 99.0), axis=0, keepdims=True)
    t = jnp.exp(v2 - v1)
    g1 = p_grp / (1.0 + t)
    g2 = p_grp * t / (1.0 + t)
    o1 = jnp.where(ei == i1, 1.0, 0.0)
    o2 = jnp.where(ei == i2, 1.0, 0.0)
    osum = o1 + o2
    before = _dot(osum.astype(BF16), utri) + cnt_ref[:, 0:1]
    rank1 = jnp.sum(o1 * before, axis=0, keepdims=True)
    rank2 = jnp.sum(o2 * before, axis=0, keepdims=True)
    cnt_ref[...] = cnt_ref[...] + jnp.sum(osum, axis=1, keepdims=True)

    def pieces(g):
        hi = g.astype(BF16).astype(F32)
        r = g - hi
        mid = r.astype(BF16).astype(F32)
        return [hi, mid, r - mid]

    rows = [i1, i2, rank1, rank2] + pieces(g1) + pieces(g2)
    return rows + [jnp.zeros((1, tm), F32)] * (ROUTE_ROWS - len(rows))


def _mix_out_kernel(*refs, rec, alpha):
    if rec:
        (a_ref, b_ref, g_ref, gn_ref, w_ref, x_ref, mod_ref, lng_ref, lnb_ref,
         rw_hi_ref, rw_lo_ref, rb_ref, utri_ref, xo_ref, route_ref, cnt_out_ref, cnt_ref) = refs
    else:
        (a_ref, b_ref, w_ref, x_ref, mod_ref, lng_ref, lnb_ref,
         rw_hi_ref, rw_lo_ref, rb_ref, utri_ref, xo_ref, route_ref, cnt_out_ref, cnt_ref) = refs
    i = pl.program_id(0)

    @pl.when(i == 0)
    def _():
        cnt_ref[...] = jnp.zeros_like(cnt_ref)

    if rec:
        o = a_ref[...] + b_ref[...]
        g = g_ref[...]
        gate = gn_ref[...] * (g * jax.nn.sigmoid(g))
        parts = []
        for h in range(C_HEADS):
            blk = o[:, h * C_KEY_DIM:(h + 1) * C_KEY_DIM]
            ms = jnp.mean(blk * blk, axis=-1, keepdims=True)
            parts.append(blk * lax.rsqrt(ms + RMS_EPS))
        lhs = (jnp.concatenate(parts, axis=1) * gate).astype(BF16)
        y = _dot(lhs, w_ref[...])
    else:
        half = a_ref.shape[1]
        y = _dot(a_ref[...], w_ref[0:half, :]) + _dot(b_ref[...], w_ref[half:, :])
    m = mod_ref[0]
    xn = _layer_norm(alpha * x_ref[...] + (1.0 + m[2:3]) * y, lng_ref[...], lnb_ref[...])
    xo_ref[...] = xn
    h_hi, h_lo = _split2(xn * (1.0 + m[4:5]) + m[3:4])
    rw_hi = rw_hi_ref[...]
    logits_t = _dot_nt(rw_hi, h_hi) + _dot_nt(rw_hi, h_lo) + _dot_nt(rw_lo_ref[...], h_hi) + rb_ref[:, 0:1]
    for k, row in enumerate(_route_rows(logits_t, utri_ref[...], cnt_ref)):
        route_ref[k:k + 1, :] = row
    cnt_out_ref[...] = cnt_ref[...]


def _router_tables(w_rg, b_rg, w_re, b_re):
    d = w_rg.shape[0]
    wt = jnp.zeros((ROUTER_ROWS, d), F32)
    wt = wt.at[0:N_GROUPS].set(w_rg.astype(F32).T).at[8:8 + N_EXPERTS].set(w_re.astype(F32).T)
    hi = wt.astype(BF16)
    lo = (wt - hi.astype(F32)).astype(BF16)
    rb = jnp.zeros((ROUTER_ROWS,), F32).at[0:N_GROUPS].set(b_rg.astype(F32))
    rb = rb.at[8:8 + N_EXPERTS].set(b_re.astype(F32).reshape(-1))
    return hi, lo, jnp.broadcast_to(rb[:, None], (ROUTER_ROWS, LANES))


def _mix_out(lhs, w_out, x, mod_l, ln_g, ln_b, router, segs, *, alpha, rec_extra=None):
    nt, d = x.shape
    rec = rec_extra is not None
    rw_hi, rw_lo, rb = router
    idx = np.arange(TM)
    utri = jnp.asarray(idx[:, None] < idx[None, :], BF16)
    row = lambda i: (i, 0)
    const = lambda i: (0, 0)
    a, b = lhs
    in_specs = [pl.BlockSpec((TM, a.shape[1]), row), pl.BlockSpec((TM, b.shape[1]), row)]
    args = [a, b]
    if rec:
        proj, gnorm = rec_extra
        in_specs += [pl.BlockSpec((TM, d), lambda i: (i, 4)), pl.BlockSpec((1, d), const)]
        args += [proj, gnorm.reshape(1, d)]
    in_specs += [pl.BlockSpec(w_out.shape, const), pl.BlockSpec((TM, d), row),
                 pl.BlockSpec((1, 6, d), lambda i: (_mod_row(i, TM, segs), 0, 0)),
                 pl.BlockSpec((1, d), const), pl.BlockSpec((1, d), const),
                 pl.BlockSpec(rw_hi.shape, const), pl.BlockSpec(rw_lo.shape, const),
                 pl.BlockSpec(rb.shape, const), pl.BlockSpec((TM, TM), const)]
    args += [w_out, x, mod_l, ln_g.reshape(1, d), ln_b.reshape(1, d), rw_hi, rw_lo, rb, utri]
    return pl.pallas_call(
        functools.partial(_mix_out_kernel, rec=rec, alpha=alpha),
        grid=(nt // TM,),
        in_specs=in_specs,
        out_specs=[pl.BlockSpec((TM, d), row), pl.BlockSpec((ROUTE_ROWS, TM), lambda i: (0, i)),
                   pl.BlockSpec((N_EXPERTS, LANES), const)],
        out_shape=[jax.ShapeDtypeStruct((nt, d), F32), jax.ShapeDtypeStruct((ROUTE_ROWS, nt), F32),
                   jax.ShapeDtypeStruct((N_EXPERTS, LANES), F32)],
        scratch_shapes=[pltpu.VMEM((N_EXPERTS, LANES), F32)],
        compiler_params=_cparams(("arbitrary",)),
        name="mix_out_rec" if rec else "mix_out_attn",
    )(*args)


def _dest_kernel(ps_ref, route_ref, o_ref):
    r = route_ref[...]
    for k in range(2):
        eid = r[k:k + 1].astype(jnp.int32)
        acc = jnp.zeros_like(eid)
        for e in range(N_EXPERTS):
            acc = jnp.where(eid == e, ps_ref[e], acc)
        o_ref[k:k + 1, :] = acc + r[2 + k:3 + k].astype(jnp.int32)


def _dest(route, pad_starts):
    nt = route.shape[1]
    tb = 2048
    return pl.pallas_call(
        _dest_kernel,
        grid_spec=pltpu.PrefetchScalarGridSpec(
            num_scalar_prefetch=1,
            grid=(nt // tb,),
            in_specs=[pl.BlockSpec((ROUTE_ROWS, tb), lambda i, ps: (0, i))],
            out_specs=pl.BlockSpec((2, tb), lambda i, ps: (0, i)),
        ),
        out_shape=jax.ShapeDtypeStruct((2, nt), jnp.int32),
        compiler_params=_cparams(("parallel",)),
        name="moe_dest",
    )(pad_starts, route)


def _dispatch_kernel(dest_ref, x_ref, mod_ref, route_ref, sel_ref, xbuf_in_ref, xbuf_ref, h1_ref, h2_ref, sem):
    del xbuf_in_ref
    m = mod_ref[0]
    h = x_ref[...] * (1.0 + m[4:5]) + m[3:4]
    gates = _dot_tn(route_ref[...].astype(BF16), sel_ref[...])
    h1_ref[:, 0:D_MODEL] = h
    h2_ref[:, 0:D_MODEL] = h
    h1_ref[:, D_MODEL:] = gates[:, 0:GATE_W]
    h2_ref[:, D_MODEL:] = gates[:, GATE_W:]

    def body(t, carry):
        pltpu.make_async_copy(h1_ref.at[pl.ds(t, 1)], xbuf_ref.at[pl.ds(dest_ref[0, t], 1)], sem.at[0]).start()
        pltpu.make_async_copy(h2_ref.at[pl.ds(t, 1)], xbuf_ref.at[pl.ds(dest_ref[1, t], 1)], sem.at[1]).start()
        return carry

    lax.fori_loop(0, TM, body, 0, unroll=8)
    pltpu.make_async_copy(h1_ref, xbuf_ref.at[pl.ds(0, TM)], sem.at[0]).wait()
    pltpu.make_async_copy(h2_ref, xbuf_ref.at[pl.ds(0, TM)], sem.at[1]).wait()


def _gate_select():
    sel = np.zeros((ROUTE_ROWS, 2 * GATE_W), np.float32)
    sel[4:7, 0:GATE_W] = 1.0
    sel[7:10, GATE_W:] = 1.0
    return jnp.asarray(sel, BF16)


def _dispatch(x, mod_l, route, dest, cap, segs):
    nt, d = x.shape
    xbuf0 = jnp.zeros((cap, XBUF_W), F32)
    return pl.pallas_call(
        _dispatch_kernel,
        grid=(nt // TM,),
        in_specs=[pl.BlockSpec((2, TM), lambda i: (0, i), memory_space=pltpu.SMEM),
                  pl.BlockSpec((TM, d), lambda i: (i, 0)),
                  pl.BlockSpec((1, 6, d), lambda i: (_mod_row(i, TM, segs), 0, 0)),
                  pl.BlockSpec((ROUTE_ROWS, TM), lambda i: (0, i)),
                  pl.BlockSpec((ROUTE_ROWS, 2 * GATE_W), lambda i: (0, 0)),
                  pl.BlockSpec(memory_space=pl.ANY)],
        out_specs=pl.BlockSpec(memory_space=pl.ANY),
        out_shape=jax.ShapeDtypeStruct((cap, XBUF_W), F32),
        scratch_shapes=[pltpu.VMEM((TM, XBUF_W), F32), pltpu.VMEM((TM, XBUF_W), F32),
                        pltpu.SemaphoreType.DMA((2,))],
        input_output_aliases={5: 0},
        compiler_params=_cparams(("arbitrary",)),
        name="moe_dispatch",
    )(dest, x, mod_l, route, _gate_select(), xbuf0)


def _expert_kernel(be_ref, nu_ref, x_ref, wg_ref, wu_ref, wd_ref, o_ref):
    i = pl.program_id(0)

    @pl.when(i < nu_ref[0])
    def _():
        xb = x_ref[:, 0:D_MODEL].astype(BF16)
        hg = _dot(xb, wg_ref[0])
        hu = _dot(xb, wu_ref[0])
        hid = (hg * jax.nn.sigmoid(hg) * hu).astype(BF16)
        y = _dot(hid, wd_ref[0])
        gate = x_ref[:, D_MODEL:]
        for c in range(D_MODEL // GATE_W):
            o_ref[:, c * GATE_W:(c + 1) * GATE_W] = y[:, c * GATE_W:(c + 1) * GATE_W] * gate

    @pl.when(i >= nu_ref[0])
    def _():
        o_ref[...] = jnp.zeros_like(o_ref)


def _experts(xbuf, blk_exp, n_used, wg, wu, wd):
    cap = xbuf.shape[0]
    n_blk = cap // MOE_BLK
    blk = lambda i, be, nu: (jnp.minimum(i, nu[0] - 1), 0)
    oblk = lambda i, be, nu: (i, 0)
    wmap = lambda i, be, nu: (be[jnp.minimum(i, nu[0] - 1)], 0, 0)
    return pl.pallas_call(
        _expert_kernel,
        grid_spec=pltpu.PrefetchScalarGridSpec(
            num_scalar_prefetch=2,
            grid=(n_blk,),
            in_specs=[pl.BlockSpec((MOE_BLK, XBUF_W), blk),
                      pl.BlockSpec((1, D_MODEL, D_EXPERT), wmap),
                      pl.BlockSpec((1, D_MODEL, D_EXPERT), wmap),
                      pl.BlockSpec((1, D_EXPERT, D_MODEL), wmap)],
            out_specs=pl.BlockSpec((MOE_BLK, D_MODEL), oblk),
        ),
        out_shape=jax.ShapeDtypeStruct((cap, D_MODEL), F32),
        compiler_params=_cparams(("arbitrary",)),
        name="moe_experts",
    )(blk_exp, n_used, xbuf, wg, wu, wd)


def _combine_kernel(dest_ref, ybuf_ref, x_ref, mod_ref, lng_ref, lnb_ref, o_ref, ya_ref, yb_ref, sem, *, alpha):
    def body(t, carry):
        pltpu.make_async_copy(ybuf_ref.at[pl.ds(dest_ref[0, t], 1)], ya_ref.at[pl.ds(t, 1)], sem.at[0]).start()
        pltpu.make_async_copy(ybuf_ref.at[pl.ds(dest_ref[1, t], 1)], yb_ref.at[pl.ds(t, 1)], sem.at[1]).start()
        return carry

    lax.fori_loop(0, TM, body, 0, unroll=8)
    pltpu.make_async_copy(ybuf_ref.at[pl.ds(0, TM)], ya_ref, sem.at[0]).wait()
    pltpu.make_async_copy(ybuf_ref.at[pl.ds(0, TM)], yb_ref, sem.at[1]).wait()
    m = mod_ref[0]
    y = ya_ref[...] + yb_ref[...]
    o_ref[...] = _layer_norm(alpha * x_ref[...] + (1.0 + m[5:6]) * y, lng_ref[...], lnb_ref[...])


def _combine(ybuf, dest, x, mod_l, ln_g, ln_b, segs, *, alpha):
    nt, d = x.shape
    const = lambda i: (0, 0)
    return pl.pallas_call(
        functools.partial(_combine_kernel, alpha=alpha),
        grid=(nt // TM,),
        in_specs=[pl.BlockSpec((2, TM), lambda i: (0, i), memory_space=pltpu.SMEM),
                  pl.BlockSpec(memory_space=pl.ANY),
                  pl.BlockSpec((TM, d), lambda i: (i, 0)),
                  pl.BlockSpec((1, 6, d), lambda i: (_mod_row(i, TM, segs), 0, 0)),
                  pl.BlockSpec((1, d), const), pl.BlockSpec((1, d), const)],
        out_specs=pl.BlockSpec((TM, d), lambda i: (i, 0)),
        out_shape=jax.ShapeDtypeStruct((nt, d), F32),
        scratch_shapes=[pltpu.VMEM((TM, d), F32), pltpu.VMEM((TM, d), F32), pltpu.SemaphoreType.DMA((2,))],
        compiler_params=_cparams(("arbitrary",)),
        name="moe_combine",
    )(dest, ybuf, x, mod_l, ln_g.reshape(1, d), ln_b.reshape(1, d))


def _moe(x, mod_l, route, counts, ln_g, ln_b, wg, wu, wd, segs, *, alpha):
    nt = x.shape[0]
    cap = 2 * nt + N_EXPERTS * MOE_BLK
    n_blk = cap // MOE_BLK
    cnt = counts[:, 0].astype(jnp.int32)
    padded = (cnt + MOE_BLK - 1) // MOE_BLK * MOE_BLK
    pad_ends = jnp.cumsum(padded)
    pad_starts = (pad_ends - padded).astype(jnp.int32)
    blk_exp = jnp.minimum(
        jnp.searchsorted(pad_ends, jnp.arange(n_blk, dtype=jnp.int32) * MOE_BLK, side="right"),
        N_EXPERTS - 1).astype(jnp.int32)
    n_used = (pad_ends[-1:] // MOE_BLK).astype(jnp.int32)
    dest = _dest(route, pad_starts)
    xbuf = _dispatch(x, mod_l, route, dest, cap, segs)
    ybuf = _experts(xbuf, blk_exp, n_used, wg, wu, wd)
    return _combine(ybuf, dest, x, mod_l, ln_g, ln_b, segs, alpha=alpha)


def _attn_w_in_layout(w_in):
    scale = HEAD_DIM ** -0.5
    qa = w_in[:, 0:512] * scale
    ka = w_in[:, 512:640]
    va = w_in[:, 640:768]
    qb = w_in[:, 768:1280] * scale
    kb = w_in[:, 1280:1792]
    vb = w_in[:, 1792:2304]
    dup = lambda a: jnp.concatenate([a[:, 0:64], a[:, 0:64], a[:, 64:128], a[:, 64:128]], axis=1)
    return jnp.concatenate([qa, dup(ka), dup(va), qb, kb, vb], axis=1).astype(BF16)


def _rope_tables(t_max):
    inv = jnp.power(ROPE_THETA, -jnp.arange(0, HEAD_DIM, 2, dtype=F32) / HEAD_DIM)
    ang = jnp.arange(t_max, dtype=F32)[:, None] * inv[None, :]
    cos = jnp.cos(ang)
    sin = jnp.sin(ang)
    cos128 = jnp.concatenate([cos, cos, cos, cos], axis=1)
    sin128 = jnp.concatenate([-sin, sin, -sin, sin], axis=1)
    return cos128, sin128


def kernel(x_prompt, x_sample, c_prompt, c_sample, ada_w, ada_b, ln_g, ln_b, attn_w_in, attn_sink, nat_rel_bias, attn_w_out, rec_w_in, rec_lb, rec_gnorm, rec_w_out, router_w_group, router_b_group, router_w_expert, router_b_expert, expert_w_gate, expert_w_up, expert_w_down):
    depth = ada_w.shape[0]
    d = x_prompt.shape[-1]
    segs = ((x_prompt.shape[0], x_prompt.shape[1]), (x_sample.shape[0], x_sample.shape[1]))
    alpha = (2 * depth) ** 0.25
    x = jnp.concatenate([x_prompt.reshape(-1, d), x_sample.reshape(-1, d)], axis=0)
    c = jnp.concatenate([c_prompt, c_sample], axis=0)
    nseq = c.shape[0]
    mod = _adaln(c, ada_w, ada_b).reshape(depth, nseq, 6, d)
    rope_tables = _rope_tables(max(t for _, t in segs))
    for l in range(depth):
        mod_l = mod[l]
        i = l // 2
        router = _router_tables(router_w_group[l], router_b_group[l], router_w_expert[l], router_b_expert[l])
        if l % 2 == 0:
            proj = _inproj(x, mod_l, _attn_w_in_layout(attn_w_in[i]), segs, tn=ATTN_W // 2, out_dtype=BF16,
                           rope_tables=rope_tables)
            oa = _win_attn(proj, attn_sink[i].astype(F32), segs)
            ob = _nat_attn(proj, _nat_bias_table(nat_rel_bias[i]), segs)
            x, route, counts = _mix_out((oa, ob), attn_w_out[i].astype(BF16), x, mod_l, ln_g[l, 0], ln_b[l, 0],
                                        router, segs, alpha=alpha)
        else:
            proj = _inproj(x, mod_l, rec_w_in[i].astype(BF16), segs, tn=D_MODEL, out_dtype=F32)
            o_f, o_b = _hgrn(proj, rec_lb.astype(F32), i, segs)
            x, route, counts = _mix_out((o_f, o_b), rec_w_out[i].astype(BF16), x, mod_l, ln_g[l, 0], ln_b[l, 0],
                                        router, segs, alpha=alpha, rec_extra=(proj, rec_gnorm[i].astype(F32)))
        x = _moe(x, mod_l, route, counts, ln_g[l, 1], ln_b[l, 1], expert_w_gate[l].astype(BF16),
                 expert_w_up[l].astype(BF16), expert_w_down[l].astype(BF16), segs, alpha=alpha)
    n0 = segs[0][0] * segs[0][1]
    return (x[:n0].reshape(x_prompt.shape), x[n0:].reshape(x_sample.shape))
```

```python
import functools

import numpy as np
import jax
import jax.numpy as jnp
from jax import lax
from jax.experimental import pallas as pl
from jax.experimental.pallas import tpu as pltpu

F32 = jnp.float32
BF16 = jnp.bfloat16
NEG_INF = float("-inf")

D_MODEL = 1024
HEAD_DIM = 64
A_HEADS = 8
A_KV_HEADS = 2
A_WINDOW = 128
A_BLOCK = 128
B_HEADS = 8
GRID_W = 64
NB_ROWS = 8
NB_COLS = 16
ROPE_THETA = 10000.0
C_HEADS = 8
C_KEY_DIM = 128
C_CHUNK = 64
N_GROUPS = 4
EXPERTS_PER_GROUP = 8
N_EXPERTS = 32
D_EXPERT = 512
LN_EPS = 1e-5
RMS_EPS = 1e-6
LOG2E = 1.4426950408889634

LANES = 128
ATTN_W = 2560
ROPE_W = 768
REC_W = 5 * D_MODEL
ROUTER_ROWS = 40
ROUTE_ROWS = 16
GATE_W = LANES
XBUF_W = D_MODEL + GATE_W

TM = 512
TM_IN = 1024
MOE_BLK = 256
NAT_BLK = 512
REC_BLK = 512
VMEM_LIMIT = 48 * 1024 * 1024


def _cparams(sem):
    return pltpu.CompilerParams(dimension_semantics=sem, vmem_limit_bytes=VMEM_LIMIT)


def _decode(i, blk, segs):
    out = None
    blk0 = 0
    for (b, t) in segs:
        nb = t // blk
        j = (i - blk0) % nb
        cand = (i - j, j, nb)
        out = cand if out is None else tuple(jnp.where(i < blk0, o, c) for o, c in zip(out, cand))
        blk0 += b * nb
    return out


def _mod_row(i, blk, segs):
    out = None
    blk0 = 0
    row0 = 0
    for (b, t) in segs:
        nb = t // blk
        cand = row0 + (i - blk0) // nb
        out = cand if out is None else jnp.where(i < blk0, out, cand)
        blk0 += b * nb
        row0 += b
    return out


def _pos_block(i, blk, segs):
    return _decode(i, blk, segs)[1]


def _dot(a, b):
    return jnp.dot(a, b, preferred_element_type=F32)


def _dot_nt(a, b):
    return lax.dot_general(a, b, (((1,), (1,)), ((), ())), preferred_element_type=F32)


def _dot_tn(a, b):
    return lax.dot_general(a, b, (((0,), (0,)), ((), ())), preferred_element_type=F32)


def _split2(x):
    hi = x.astype(BF16)
    lo = (x - hi.astype(F32)).astype(BF16)
    return hi, lo


def _layer_norm(x, g, b):
    mu = jnp.mean(x, axis=-1, keepdims=True)
    xc = x - mu
    var = jnp.mean(xc * xc, axis=-1, keepdims=True)
    return xc * lax.rsqrt(var + LN_EPS) * g + b


def _adaln_kernel(c_ref, w_ref, b_ref, o_ref):
    c = c_ref[...]
    cs = c * jax.nn.sigmoid(c)
    c_hi, c_lo = _split2(cs)
    w_hi, w_lo = _split2(w_ref[0])
    o_ref[0] = _dot(c_hi, w_hi) + _dot(c_hi, w_lo) + _dot(c_lo, w_hi) + b_ref[0]


def _adaln(c, ada_w, ada_b):
    depth, d, n = ada_w.shape
    nb = c.shape[0]
    tn = 1536
    return pl.pallas_call(
        _adaln_kernel,
        grid=(depth, n // tn),
        in_specs=[
            pl.BlockSpec((nb, d), lambda l, j: (0, 0)),
            pl.BlockSpec((1, d, tn), lambda l, j: (l, 0, j)),
            pl.BlockSpec((1, 1, tn), lambda l, j: (l, 0, j)),
        ],
        out_specs=pl.BlockSpec((1, nb, tn), lambda l, j: (l, 0, j)),
        out_shape=jax.ShapeDtypeStruct((depth, nb, n), F32),
        compiler_params=_cparams(("parallel", "parallel")),
        name="adaln",
    )(c, ada_w, ada_b.reshape(depth, 1, n))


def _rot_half_pairs(x):
    lane = lax.broadcasted_iota(jnp.int32, x.shape, 1)
    first = (lane & 63) < 32
    return jnp.where(first, pltpu.roll(x, 96, 1), pltpu.roll(x, 32, 1))


def _modulate_once(x_ref, mod_ref, xb_ref):
    @pl.when(pl.program_id(1) == 0)
    def _():
        m = mod_ref[0]
        xb_ref[...] = (x_ref[...] * (1.0 + m[1:2]) + m[0:1]).astype(BF16)


def _inproj_attn_kernel(x_ref, mod_ref, w_ref, cos_ref, sin_ref, o_ref, xb_ref):
    _modulate_once(x_ref, mod_ref, xb_ref)
    y = _dot(xb_ref[...], w_ref[...])

    @pl.when(pl.program_id(1) == 0)
    def _():
        cos = cos_ref[...]
        sin = sin_ref[...]
        for g in range(ROPE_W // LANES):
            blk = y[:, g * LANES:(g + 1) * LANES]
            o_ref[:, g * LANES:(g + 1) * LANES] = (blk * cos + _rot_half_pairs(blk) * sin).astype(o_ref.dtype)
        o_ref[:, ROPE_W:] = y[:, ROPE_W:].astype(o_ref.dtype)

    @pl.when(pl.program_id(1) != 0)
    def _():
        o_ref[...] = y.astype(o_ref.dtype)


def _inproj_rec_kernel(x_ref, mod_ref, w_ref, oz_ref, oq_ref, xb_ref, *, nz):
    _modulate_once(x_ref, mod_ref, xb_ref)
    y = _dot(xb_ref[...], w_ref[...])

    @pl.when(pl.program_id(1) < nz)
    def _():
        oz_ref[...] = y

    @pl.when(pl.program_id(1) >= nz)
    def _():
        oq_ref[...] = y.astype(oq_ref.dtype)


def _inproj_specs(x, segs, tn):
    d = x.shape[1]
    return [pl.BlockSpec((TM_IN, d), lambda i, j: (i, 0)),
            pl.BlockSpec((1, 6, d), lambda i, j: (_mod_row(i, TM_IN, segs), 0, 0)),
            pl.BlockSpec((d, tn), lambda i, j: (0, j))]


def _inproj_attn(x, mod_l, w, segs, rope_tables):
    nt, d = x.shape
    n = w.shape[1]
    tn = n // 2
    tab = pl.BlockSpec((TM_IN, LANES), lambda i, j: (_pos_block(i, TM_IN, segs), 0))
    return pl.pallas_call(
        _inproj_attn_kernel,
        grid=(nt // TM_IN, n // tn),
        in_specs=_inproj_specs(x, segs, tn) + [tab, tab],
        out_specs=pl.BlockSpec((TM_IN, tn), lambda i, j: (i, j)),
        out_shape=jax.ShapeDtypeStruct((nt, n), BF16),
        scratch_shapes=[pltpu.VMEM((TM_IN, d), BF16)],
        compiler_params=_cparams(("parallel", "arbitrary")),
        name="inproj_rope",
    )(x, mod_l, w, *rope_tables)


def _inproj_rec(x, mod_l, w, segs):
    nt, d = x.shape
    tn = D_MODEL
    nz = 2
    nq = w.shape[1] // tn - nz
    return pl.pallas_call(
        functools.partial(_inproj_rec_kernel, nz=nz),
        grid=(nt // TM_IN, nz + nq),
        in_specs=_inproj_specs(x, segs, tn),
        out_specs=[pl.BlockSpec((TM_IN, tn), lambda i, j: (i, jnp.minimum(j, nz - 1))),
                   pl.BlockSpec((TM_IN, tn), lambda i, j: (i, jnp.maximum(j - nz, 0)))],
        out_shape=[jax.ShapeDtypeStruct((nt, nz * tn), F32), jax.ShapeDtypeStruct((nt, nq * tn), BF16)],
        scratch_shapes=[pltpu.VMEM((TM_IN, d), BF16)],
        compiler_params=_cparams(("parallel", "arbitrary")),
        name="inproj_rec",
    )(x, mod_l, w)


def _win_attn_kernel(sink_ref, q_ref, k0_ref, k1_ref, k2_ref, v0_ref, v1_ref, v2_ref, o_ref, *, segs):
    i = pl.program_id(0)
    _, j, nb = _decode(i, A_BLOCK, segs)
    ws = jnp.clip(j - 1, 0, nb - 3)
    span = 3 * A_BLOCK
    qpos = j * A_BLOCK + lax.broadcasted_iota(jnp.int32, (A_BLOCK, span), 0)
    kpos = ws * A_BLOCK + lax.broadcasted_iota(jnp.int32, (A_BLOCK, span), 1)
    valid = jnp.abs(qpos - kpos) <= A_WINDOW
    lo = lax.broadcasted_iota(jnp.int32, (A_BLOCK, LANES), 1) < HEAD_DIM
    kfull = jnp.concatenate([k0_ref[...], k1_ref[...], k2_ref[...]], axis=0)
    vfull = jnp.concatenate([v0_ref[...], v1_ref[...], v2_ref[...]], axis=0)
    zero = jnp.zeros((A_BLOCK, LANES), BF16)
    grp = A_HEADS // A_KV_HEADS
    for kvh in range(A_KV_HEADS):
        kk = kfull[:, kvh * LANES:(kvh + 1) * LANES]
        vv = vfull[:, kvh * LANES:(kvh + 1) * LANES]
        parts = []
        for p in range(grp // 2):
            c0 = (kvh * (grp // 2) + p) * LANES
            qp = q_ref[:, c0:c0 + LANES]
            parts.append(jnp.where(lo, qp, zero))
            parts.append(jnp.where(lo, zero, qp))
        qs = jnp.concatenate(parts, axis=0)
        s = _dot_nt(qs, kk)
        probs = []
        inv = []
        for h in range(grp):
            sh = jnp.where(valid, s[h * A_BLOCK:(h + 1) * A_BLOCK], NEG_INF)
            sink = sink_ref[kvh * grp + h]
            m = jnp.maximum(jnp.max(sh, axis=-1, keepdims=True), sink)
            e = jnp.exp(sh - m)
            den = jnp.sum(e, axis=-1, keepdims=True) + jnp.exp(sink - m)
            probs.append(e.astype(BF16))
            inv.append(1.0 / den)
        o = _dot(jnp.concatenate(probs, axis=0), vv)
        for p in range(grp // 2):
            oe = o[(2 * p) * A_BLOCK:(2 * p + 1) * A_BLOCK] * inv[2 * p]
            oo = o[(2 * p + 1) * A_BLOCK:(2 * p + 2) * A_BLOCK] * inv[2 * p + 1]
            c0 = (kvh * (grp // 2) + p) * LANES
            o_ref[:, c0:c0 + LANES] = jnp.where(lo, oe, oo).astype(o_ref.dtype)


def _win_attn(proj, sink, segs):
    nt = proj.shape[0]
    kvw = 2 * LANES

    def kv_map(d, col):
        def f(i):
            base, j, nb = _decode(i, A_BLOCK, segs)
            return (base + jnp.clip(j - 1, 0, nb - 3) + d, col)
        return f

    in_specs = [pl.BlockSpec(memory_space=pltpu.SMEM),
                pl.BlockSpec((A_BLOCK, 512), lambda i: (i, 0))]
    in_specs += [pl.BlockSpec((A_BLOCK, kvw), kv_map(d, 2)) for d in range(3)]
    in_specs += [pl.BlockSpec((A_BLOCK, kvw), kv_map(d, 3)) for d in range(3)]
    return pl.pallas_call(
        functools.partial(_win_attn_kernel, segs=segs),
        grid=(nt // A_BLOCK,),
        in_specs=in_specs,
        out_specs=pl.BlockSpec((A_BLOCK, 512), lambda i: (i, 0)),
        out_shape=jax.ShapeDtypeStruct((nt, 512), BF16),
        compiler_params=_cparams(("parallel",)),
        name="win_attn",
    )(sink, proj, proj, proj, proj, proj, proj, proj)


def _nat_bias_table(rel_bias):
    c = np.arange(GRID_W)
    kc = np.arange(GRID_W)
    qwin = np.clip(c - NB_COLS // 2, 0, GRID_W - NB_COLS)
    ok = (kc[None, :] >= qwin[:, None]) & (kc[None, :] < qwin[:, None] + NB_COLS)
    dc = np.clip(kc[None, :] - c[:, None], -(NB_COLS - 1), NB_COLS - 1) + NB_COLS - 1
    var = np.arange(NB_ROWS)
    irow = np.arange(NB_ROWS)
    dr = irow[None, :] - var[:, None] + NB_ROWS - 1
    t = rel_bias.astype(F32)[:, dr][:, :, :, dc]
    t = jnp.where(jnp.asarray(ok)[None, None, None], t, NEG_INF)
    t = t.transpose(1, 0, 3, 2, 4)
    return t.reshape(NB_ROWS, B_HEADS, GRID_W, NB_ROWS * GRID_W)


def _nat_attn_kernel(q_ref, k0, k1, k2, k3, v0, v1, v2, v3, bias_ref, o_ref, kbuf, vbuf, *, segs):
    i = pl.program_id(0)
    _, j, nb = _decode(i, NAT_BLK, segs)
    rows = nb * NB_ROWS
    w0 = jnp.clip(NB_ROWS * j - NB_ROWS // 2, 0, rows - 2 * NB_ROWS)
    for d, (kr, vr) in enumerate(((k0, v0), (k1, v1), (k2, v2), (k3, v3))):
        kbuf[d * 256:(d + 1) * 256, :] = kr[...]
        vbuf[d * 256:(d + 1) * 256, :] = vr[...]
    lo = lax.broadcasted_iota(jnp.int32, (GRID_W, LANES), 1) < HEAD_DIM
    zero = jnp.zeros((GRID_W, LANES), BF16)
    nkeys = NB_ROWS * GRID_W
    for rr in range(NB_ROWS):
        r = NB_ROWS * j + rr
        rs = jnp.clip(r - NB_ROWS // 2, 0, rows - NB_ROWS)
        var = r - rs
        off = pl.multiple_of((rs - w0) * GRID_W, GRID_W)
        for hp in range(B_HEADS // 2):
            cs = slice(hp * LANES, (hp + 1) * LANES)
            kp = kbuf[pl.ds(off, nkeys), cs]
            vp = vbuf[pl.ds(off, nkeys), cs]
            qp = q_ref[rr * GRID_W:(rr + 1) * GRID_W, cs]
            qs = jnp.concatenate([jnp.where(lo, qp, zero), jnp.where(lo, zero, qp)], axis=0)
            s = _dot_nt(qs, kp)
            bias = jnp.concatenate([bias_ref[var, 2 * hp], bias_ref[var, 2 * hp + 1]], axis=0)
            s = s + bias
            m = jnp.max(s, axis=-1, keepdims=True)
            e = jnp.exp(s - m)
            inv = 1.0 / jnp.sum(e, axis=-1, keepdims=True)
            o = _dot(e.astype(BF16), vp) * inv
            o_ref[rr * GRID_W:(rr + 1) * GRID_W, cs] = jnp.where(lo, o[:GRID_W], o[GRID_W:]).astype(o_ref.dtype)


def _nat_attn(proj, bias_table, segs):
    nt = proj.shape[0]

    def kv_map(d, col):
        def f(i):
            base, j, nb = _decode(i, NAT_BLK, segs)
            return (2 * base + jnp.clip(2 * j - 1, 0, 2 * nb - 4) + d, col)
        return f

    in_specs = [pl.BlockSpec((NAT_BLK, 512), lambda i: (i, 2))]
    in_specs += [pl.BlockSpec((256, 512), kv_map(d, 3)) for d in range(4)]
    in_specs += [pl.BlockSpec((256, 512), kv_map(d, 4)) for d in range(4)]
    in_specs += [pl.BlockSpec(bias_table.shape, lambda i: (0, 0, 0, 0))]
    return pl.pallas_call(
        functools.partial(_nat_attn_kernel, segs=segs),
        grid=(nt // NAT_BLK,),
        in_specs=in_specs,
        out_specs=pl.BlockSpec((NAT_BLK, 512), lambda i: (i, 0)),
        out_shape=jax.ShapeDtypeStruct((nt, 512), BF16),
        scratch_shapes=[pltpu.VMEM((1024, 512), BF16), pltpu.VMEM((1024, 512), BF16)],
        compiler_params=_cparams(("parallel",)),
        name="nat_attn",
    )(*([proj] * 9), bias_table)


def _chunk_cumsum(x, fwd):
    n = x.shape[0]
    x3 = x.reshape(n // 8, 8, LANES)
    sub = lax.broadcasted_iota(jnp.int32, x3.shape, 1)
    s = 1
    while s < 8:
        if fwd:
            x3 = x3 + jnp.where(sub >= s, pltpu.roll(x3, s, 1), 0.0)
        else:
            x3 = x3 + jnp.where(sub < 8 - s, pltpu.roll(x3, 8 - s, 1), 0.0)
        s *= 2
    nv = n // 8
    outs = [None] * nv
    edges = [None] * nv
    run = None
    edge = 7 if fwd else 0
    for j in (range(nv) if fwd else range(nv - 1, -1, -1)):
        cur = x3[j]
        if run is not None:
            cur = cur + run
        outs[j] = cur
        run = jnp.broadcast_to(cur[edge:edge + 1, :], (8, LANES))
        edges[j] = run
    return outs, edges


def _split_rows(outs, edges, m, fwd):
    nv = len(outs)
    if m >= 8:
        g = m // 8
        res = []
        for j in range(nv):
            first = (j // (2 * g)) * 2 * g
            res.append(edges[first + g - 1] if fwd else edges[first + g])
        return res
    sub = lax.broadcasted_iota(jnp.int32, (8, LANES), 0)
    pick = m - 1 if fwd else m
    res = []
    for j in range(nv):
        r = None
        for gi in reversed(range(8 // (2 * m))):
            row = gi * 2 * m + pick
            bc = jnp.broadcast_to(outs[j][row:row + 1, :], (8, LANES))
            r = bc if r is None else jnp.where(sub < (gi + 1) * 2 * m, bc, r)
        res.append(r)
    return res


def _hgrn_chunk(q, z, vb, lower, masks, st_ref, fwd):
    n = C_CHUNK
    f = lower + (1.0 - lower) * jax.nn.sigmoid(z)
    kk = 1.0 - f
    outs, edges = _chunk_cumsum(jnp.log(f) * LOG2E, fwd)
    b = jnp.concatenate(outs, axis=0)
    total = b[n - 1:n] if fwd else b[0:1]
    rowi = lax.broadcasted_iota(jnp.int32, (n, 1), 0)
    zero = jnp.zeros((n, LANES), BF16)
    xs = []
    m = n // 2
    while m >= 1:
        r = jnp.concatenate(_split_rows(outs, edges, m, fwd), axis=0)
        upper = (rowi & (2 * m - 1)) >= m
        qside = upper if fwd else jnp.logical_not(upper)
        e = jnp.exp2(jnp.where(qside, b - r, r - b))
        xs.append((jnp.where(qside, q, kk) * e).astype(BF16))
        m //= 2
    ti = lax.broadcasted_iota(jnp.int32, (n, LANES), 0)
    li = lax.broadcasted_iota(jnp.int32, (n, LANES), 1)
    asum = jnp.where(ti == li, jnp.sum(q * kk, axis=-1, keepdims=True), 0.0)
    for p in range(len(xs) // 2):
        x1, x2 = xs[2 * p], xs[2 * p + 1]
        lhs = jnp.concatenate([x1, x2], axis=1)
        rhs = jnp.concatenate([jnp.concatenate([x1, zero], axis=1), jnp.concatenate([zero, x2], axis=1)], axis=0)
        asum = asum + _dot_nt(lhs, rhs) * masks[p]
    st = st_ref[...]
    o = _dot(asum.astype(BF16), jnp.concatenate([vb, vb], axis=0))
    o = o + _dot_nt((q * jnp.exp2(b)).astype(BF16), st.astype(BF16))
    ke = (kk * jnp.exp2(total - b)).astype(BF16)
    st_ref[...] = st * jnp.exp2(total) + _dot_tn(vb, ke)
    return o


def _level_masks(fwd):
    n = C_CHUNK
    t = np.arange(n)[:, None]
    s = np.arange(n)[None, :]
    out = np.zeros((3, n, 2 * n), np.float32)
    for i, m in enumerate([32, 16, 8, 4, 2, 1]):
        same = (t // (2 * m)) == (s // (2 * m))
        tu = (t % (2 * m)) >= m
        su = (s % (2 * m)) >= m
        ok = same & ((tu & ~su) if fwd else (~tu & su))
        out[i // 2, :, (i % 2) * n:(i % 2 + 1) * n] = ok
    return out


def _hgrn_kernel(lb_ref, mf_ref, mb_ref, qf_ref, zf_ref, vf_ref, qb_ref, zb_ref, vb_ref,
                 of_ref, ob_ref, sf_ref, sb_ref, *, segs, layer):
    i = pl.program_id(1)
    _, j, _ = _decode(i, REC_BLK, segs)

    @pl.when(j == 0)
    def _():
        sf_ref[...] = jnp.zeros_like(sf_ref)
        sb_ref[...] = jnp.zeros_like(sb_ref)

    lb = lb_ref[...]
    ex = jnp.exp(lb - jnp.max(lb, axis=0, keepdims=True))
    p = ex / jnp.sum(ex, axis=0, keepdims=True)
    lower = jnp.sum(p[0:layer + 1], axis=0, keepdims=True) - p[0:1]
    mf = [mf_ref[k] for k in range(3)]
    mb = [mb_ref[k] for k in range(3)]
    nch = REC_BLK // C_CHUNK
    for c in range(nch):
        sl = slice(c * C_CHUNK, (c + 1) * C_CHUNK)
        of_ref[sl, :] = _hgrn_chunk(qf_ref[sl, :].astype(F32), zf_ref[sl, :], vf_ref[sl, :], lower, mf, sf_ref, True)
        cb = nch - 1 - c
        sl = slice(cb * C_CHUNK, (cb + 1) * C_CHUNK)
        ob_ref[sl, :] = _hgrn_chunk(qb_ref[sl, :].astype(F32), zb_ref[sl, :], vb_ref[sl, :], lower, mb, sb_ref, False)


def _hgrn(proj_z, proj_qvg, rec_lb, layer, segs):
    nt = proj_z.shape[0]
    nrec = rec_lb.shape[0]
    mf = jnp.asarray(_level_masks(True))
    mb = jnp.asarray(_level_masks(False))

    def fwd_map(col):
        return lambda h, i: (i, col * C_HEADS + h)

    def bwd_map(col):
        def f(h, i):
            base, j, nb = _decode(i, REC_BLK, segs)
            return (base + nb - 1 - j, col * C_HEADS + h)
        return f

    blk = (REC_BLK, C_KEY_DIM)
    c3 = lambda h, i: (0, 0, 0)
    in_specs = [pl.BlockSpec((nrec, C_KEY_DIM), lambda h, i: (0, h)),
                pl.BlockSpec(mf.shape, c3), pl.BlockSpec(mb.shape, c3),
                pl.BlockSpec(blk, fwd_map(0)), pl.BlockSpec(blk, fwd_map(0)), pl.BlockSpec(blk, fwd_map(1)),
                pl.BlockSpec(blk, bwd_map(0)), pl.BlockSpec(blk, bwd_map(1)), pl.BlockSpec(blk, bwd_map(1))]
    out_shape = [jax.ShapeDtypeStruct((nt, D_MODEL), F32)] * 2
    return pl.pallas_call(
        functools.partial(_hgrn_kernel, segs=segs, layer=layer),
        grid=(C_HEADS, nt // REC_BLK),
        in_specs=in_specs,
        out_specs=[pl.BlockSpec(blk, fwd_map(0)), pl.BlockSpec(blk, bwd_map(0))],
        out_shape=out_shape,
        scratch_shapes=[pltpu.VMEM((C_KEY_DIM, C_KEY_DIM), F32)] * 2,
        compiler_params=_cparams(("parallel", "arbitrary")),
        name="hgrn",
    )(rec_lb, mf, mb, proj_qvg, proj_z, proj_qvg, proj_qvg, proj_z, proj_qvg)


def _route_rows(logits_t, utri, cnt_ref):
    tm = logits_t.shape[1]
    gi = lax.broadcasted_iota(jnp.int32, (8, tm), 0).astype(F32)
    lg = jnp.where(gi < N_GROUPS, logits_t[0:8], NEG_INF)
    gm = jnp.max(lg, axis=0, keepdims=True)
    p_grp = 1.0 / jnp.sum(jnp.exp(lg - gm), axis=0, keepdims=True)
    grp = jnp.min(jnp.where(lg == gm, gi, 99.0), axis=0, keepdims=True)
    ei = lax.broadcasted_iota(jnp.int32, (N_EXPERTS, tm), 0).astype(F32)
    egrp = (lax.broadcasted_iota(jnp.int32, (N_EXPERTS, tm), 0) >> 3).astype(F32)
    le = jnp.where(egrp == grp, logits_t[8:8 + N_EXPERTS], NEG_INF)
    v1 = jnp.max(le, axis=0, keepdims=True)
    i1 = jnp.min(jnp.where(le == v1, ei, 99.0), axis=0, keepdims=True)
    le2 = jnp.where(ei == i1, NEG_INF, le)
    v2 = jnp.max(le2, axis=0, keepdims=True)
    i2 = jnp.min(jnp.where(le2 == v2, ei, 99.0), axis=0, keepdims=True)
    t = jnp.exp(v2 - v1)
    g1 = p_grp / (1.0 + t)
    g2 = p_grp * t / (1.0 + t)
    o1 = jnp.where(ei == i1, 1.0, 0.0)
    o2 = jnp.where(ei == i2, 1.0, 0.0)
    osum = o1 + o2
    before = _dot(osum.astype(BF16), utri) + cnt_ref[:, 0:1]
    rank1 = jnp.sum(o1 * before, axis=0, keepdims=True)
    rank2 = jnp.sum(o2 * before, axis=0, keepdims=True)
    cnt_ref[...] = cnt_ref[...] + jnp.sum(osum, axis=1, keepdims=True)

    def pieces(g):
        hi = g.astype(BF16).astype(F32)
        r = g - hi
        mid = r.astype(BF16).astype(F32)
        return [hi, mid, r - mid]

    rows = [i1, i2, rank1, rank2] + pieces(g1) + pieces(g2)
    return rows + [jnp.zeros((1, tm), F32)] * (ROUTE_ROWS - len(rows))


def _mix_out_kernel(*refs, rec, alpha):
    if rec:
        (a_ref, b_ref, g_ref, gn_ref, w_ref, x_ref, mod_ref, lng_ref, lnb_ref,
         rw_hi_ref, rw_lo_ref, rb_ref, utri_ref, xo_ref, route_ref, cnt_out_ref, cnt_ref) = refs
    else:
        (a_ref, b_ref, w_ref, x_ref, mod_ref, lng_ref, lnb_ref,
         rw_hi_ref, rw_lo_ref, rb_ref, utri_ref, xo_ref, route_ref, cnt_out_ref, cnt_ref) = refs
    i = pl.program_id(0)

    @pl.when(i == 0)
    def _():
        cnt_ref[...] = jnp.zeros_like(cnt_ref)

    if rec:
        o = a_ref[...] + b_ref[...]
        g = g_ref[...].astype(F32)
        gate = gn_ref[...] * (g * jax.nn.sigmoid(g))
        parts = []
        for h in range(C_HEADS):
            blk = o[:, h * C_KEY_DIM:(h + 1) * C_KEY_DIM]
            ms = jnp.mean(blk * blk, axis=-1, keepdims=True)
            parts.append(blk * lax.rsqrt(ms + RMS_EPS))
        lhs = (jnp.concatenate(parts, axis=1) * gate).astype(BF16)
        y = _dot(lhs, w_ref[...])
    else:
        half = a_ref.shape[1]
        y = _dot(a_ref[...], w_ref[0:half, :]) + _dot(b_ref[...], w_ref[half:, :])
    m = mod_ref[0]
    xn = _layer_norm(alpha * x_ref[...] + (1.0 + m[2:3]) * y, lng_ref[...], lnb_ref[...])
    xo_ref[...] = xn
    h_hi, h_lo = _split2(xn * (1.0 + m[4:5]) + m[3:4])
    rw_hi = rw_hi_ref[...]
    logits_t = _dot_nt(rw_hi, h_hi) + _dot_nt(rw_hi, h_lo) + _dot_nt(rw_lo_ref[...], h_hi) + rb_ref[:, 0:1]
    for k, row in enumerate(_route_rows(logits_t, utri_ref[...], cnt_ref)):
        route_ref[k:k + 1, :] = row
    cnt_out_ref[...] = cnt_ref[...]


def _router_tables(w_rg, b_rg, w_re, b_re):
    d = w_rg.shape[0]
    wt = jnp.zeros((ROUTER_ROWS, d), F32)
    wt = wt.at[0:N_GROUPS].set(w_rg.astype(F32).T).at[8:8 + N_EXPERTS].set(w_re.astype(F32).T)
    hi = wt.astype(BF16)
    lo = (wt - hi.astype(F32)).astype(BF16)
    rb = jnp.zeros((ROUTER_ROWS,), F32).at[0:N_GROUPS].set(b_rg.astype(F32))
    rb = rb.at[8:8 + N_EXPERTS].set(b_re.astype(F32).reshape(-1))
    return hi, lo, jnp.broadcast_to(rb[:, None], (ROUTER_ROWS, LANES))


def _mix_out(lhs, w_out, x, mod_l, ln_g, ln_b, router, segs, *, alpha, rec_extra=None):
    nt, d = x.shape
    rec = rec_extra is not None
    rw_hi, rw_lo, rb = router
    idx = np.arange(TM)
    utri = jnp.asarray(idx[:, None] < idx[None, :], BF16)
    row = lambda i: (i, 0)
    const = lambda i: (0, 0)
    a, b = lhs
    in_specs = [pl.BlockSpec((TM, a.shape[1]), row), pl.BlockSpec((TM, b.shape[1]), row)]
    args = [a, b]
    if rec:
        proj, gnorm = rec_extra
        in_specs += [pl.BlockSpec((TM, d), lambda i: (i, 2)), pl.BlockSpec((1, d), const)]
        args += [proj, gnorm.reshape(1, d)]
    in_specs += [pl.BlockSpec(w_out.shape, const), pl.BlockSpec((TM, d), row),
                 pl.BlockSpec((1, 6, d), lambda i: (_mod_row(i, TM, segs), 0, 0)),
                 pl.BlockSpec((1, d), const), pl.BlockSpec((1, d), const),
                 pl.BlockSpec(rw_hi.shape, const), pl.BlockSpec(rw_lo.shape, const),
                 pl.BlockSpec(rb.shape, const), pl.BlockSpec((TM, TM), const)]
    args += [w_out, x, mod_l, ln_g.reshape(1, d), ln_b.reshape(1, d), rw_hi, rw_lo, rb, utri]
    return pl.pallas_call(
        functools.partial(_mix_out_kernel, rec=rec, alpha=alpha),
        grid=(nt // TM,),
        in_specs=in_specs,
        out_specs=[pl.BlockSpec((TM, d), row), pl.BlockSpec((ROUTE_ROWS, TM), lambda i: (0, i)),
                   pl.BlockSpec((N_EXPERTS, LANES), const)],
        out_shape=[jax.ShapeDtypeStruct((nt, d), F32), jax.ShapeDtypeStruct((ROUTE_ROWS, nt), F32),
                   jax.ShapeDtypeStruct((N_EXPERTS, LANES), F32)],
        scratch_shapes=[pltpu.VMEM((N_EXPERTS, LANES), F32)],
        compiler_params=_cparams(("arbitrary",)),
        name="mix_out_rec" if rec else "mix_out_attn",
    )(*args)


def _dest_kernel(ps_ref, route_ref, o_ref):
    r = route_ref[...]
    for k in range(2):
        eid = r[k:k + 1].astype(jnp.int32)
        acc = jnp.zeros_like(eid)
        for e in range(N_EXPERTS):
            acc = jnp.where(eid == e, ps_ref[e], acc)
        o_ref[k:k + 1, :] = acc + r[2 + k:3 + k].astype(jnp.int32)


def _dest(route, pad_starts):
    nt = route.shape[1]
    tb = 2048
    return pl.pallas_call(
        _dest_kernel,
        grid_spec=pltpu.PrefetchScalarGridSpec(
            num_scalar_prefetch=1,
            grid=(nt // tb,),
            in_specs=[pl.BlockSpec((ROUTE_ROWS, tb), lambda i, ps: (0, i))],
            out_specs=pl.BlockSpec((2, tb), lambda i, ps: (0, i)),
        ),
        out_shape=jax.ShapeDtypeStruct((2, nt), jnp.int32),
        compiler_params=_cparams(("parallel",)),
        name="moe_dest",
    )(pad_starts, route)


def _dispatch_kernel(dest_ref, x_ref, mod_ref, route_ref, sel_ref, xbuf_in_ref, xbuf_ref, h1_ref, h2_ref, sem):
    del xbuf_in_ref
    m = mod_ref[0]
    h = x_ref[...] * (1.0 + m[4:5]) + m[3:4]
    gates = _dot_tn(route_ref[...].astype(BF16), sel_ref[...])
    h1_ref[:, 0:D_MODEL] = h
    h2_ref[:, 0:D_MODEL] = h
    h1_ref[:, D_MODEL:] = gates[:, 0:GATE_W]
    h2_ref[:, D_MODEL:] = gates[:, GATE_W:]

    def body(t, carry):
        pltpu.make_async_copy(h1_ref.at[pl.ds(t, 1)], xbuf_ref.at[pl.ds(dest_ref[0, t], 1)], sem.at[0]).start(priority=0)
        pltpu.make_async_copy(h2_ref.at[pl.ds(t, 1)], xbuf_ref.at[pl.ds(dest_ref[1, t], 1)], sem.at[1]).start(priority=1)
        return carry

    lax.fori_loop(0, TM, body, 0, unroll=8)
    pltpu.make_async_copy(h1_ref, xbuf_ref.at[pl.ds(0, TM)], sem.at[0]).wait()
    pltpu.make_async_copy(h2_ref, xbuf_ref.at[pl.ds(0, TM)], sem.at[1]).wait()


def _gate_select():
    sel = np.zeros((ROUTE_ROWS, 2 * GATE_W), np.float32)
    sel[4:7, 0:GATE_W] = 1.0
    sel[7:10, GATE_W:] = 1.0
    return jnp.asarray(sel, BF16)


def _dispatch(x, mod_l, route, dest, cap, segs):
    nt, d = x.shape
    xbuf0 = jnp.zeros((cap, XBUF_W), F32)
    return pl.pallas_call(
        _dispatch_kernel,
        grid=(nt // TM,),
        in_specs=[pl.BlockSpec((2, TM), lambda i: (0, i), memory_space=pltpu.SMEM),
                  pl.BlockSpec((TM, d), lambda i: (i, 0)),
                  pl.BlockSpec((1, 6, d), lambda i: (_mod_row(i, TM, segs), 0, 0)),
                  pl.BlockSpec((ROUTE_ROWS, TM), lambda i: (0, i)),
                  pl.BlockSpec((ROUTE_ROWS, 2 * GATE_W), lambda i: (0, 0)),
                  pl.BlockSpec(memory_space=pl.ANY)],
        out_specs=pl.BlockSpec(memory_space=pl.ANY),
        out_shape=jax.ShapeDtypeStruct((cap, XBUF_W), F32),
        scratch_shapes=[pltpu.VMEM((TM, XBUF_W), F32), pltpu.VMEM((TM, XBUF_W), F32),
                        pltpu.SemaphoreType.DMA((2,))],
        input_output_aliases={5: 0},
        compiler_params=_cparams(("arbitrary",)),
        name="moe_dispatch",
    )(dest, x, mod_l, route, _gate_select(), xbuf0)


def _expert_kernel(be_ref, nu_ref, x_ref, wg_ref, wu_ref, wd_ref, o_ref, wgb_ref, wub_ref, wdb_ref):
    i = pl.program_id(0)
    cur = jnp.minimum(i, nu_ref[0] - 1)
    new_expert = jnp.logical_or(i == 0, be_ref[cur] != be_ref[jnp.maximum(cur - 1, 0)])

    @pl.when(new_expert)
    def _():
        wgb_ref[...] = wg_ref[0].astype(BF16)
        wub_ref[...] = wu_ref[0].astype(BF16)
        wdb_ref[...] = wd_ref[0].astype(BF16)

    @pl.when(i < nu_ref[0])
    def _():
        xb = x_ref[:, 0:D_MODEL].astype(BF16)
        hg = _dot(xb, wgb_ref[...])
        hu = _dot(xb, wub_ref[...])
        hid = (hg * jax.nn.sigmoid(hg) * hu).astype(BF16)
        y = _dot(hid, wdb_ref[...])
        gate = x_ref[:, D_MODEL:]
        for c in range(D_MODEL // GATE_W):
            o_ref[:, c * GATE_W:(c + 1) * GATE_W] = y[:, c * GATE_W:(c + 1) * GATE_W] * gate

    @pl.when(i >= nu_ref[0])
    def _():
        o_ref[...] = jnp.zeros_like(o_ref)


def _experts(xbuf, blk_exp, n_used, wg, wu, wd):
    cap = xbuf.shape[0]
    n_blk = cap // MOE_BLK
    blk = lambda i, be, nu: (jnp.minimum(i, nu[0] - 1), 0)
    oblk = lambda i, be, nu: (i, 0)
    wmap = lambda i, be, nu: (be[jnp.minimum(i, nu[0] - 1)], 0, 0)
    return pl.pallas_call(
        _expert_kernel,
        grid_spec=pltpu.PrefetchScalarGridSpec(
            num_scalar_prefetch=2,
            grid=(n_blk,),
            in_specs=[pl.BlockSpec((MOE_BLK, XBUF_W), blk),
                      pl.BlockSpec((1, D_MODEL, D_EXPERT), wmap),
                      pl.BlockSpec((1, D_MODEL, D_EXPERT), wmap),
                      pl.BlockSpec((1, D_EXPERT, D_MODEL), wmap)],
            out_specs=pl.BlockSpec((MOE_BLK, D_MODEL), oblk),
            scratch_shapes=[pltpu.VMEM((D_MODEL, D_EXPERT), BF16), pltpu.VMEM((D_MODEL, D_EXPERT), BF16),
                            pltpu.VMEM((D_EXPERT, D_MODEL), BF16)],
        ),
        out_shape=jax.ShapeDtypeStruct((cap, D_MODEL), F32),
        compiler_params=_cparams(("arbitrary",)),
        name="moe_experts",
    )(blk_exp, n_used, xbuf, wg, wu, wd)


def _combine_kernel(dest_ref, ybuf_ref, x_ref, mod_ref, lng_ref, lnb_ref, *rest, alpha, split_blk):
    o_refs, (ya_ref, yb_ref, sem) = rest[:-3], rest[-3:]

    def body(t, carry):
        pltpu.make_async_copy(ybuf_ref.at[pl.ds(dest_ref[0, t], 1)], ya_ref.at[pl.ds(t, 1)], sem.at[0]).start(priority=0)
        pltpu.make_async_copy(ybuf_ref.at[pl.ds(dest_ref[1, t], 1)], yb_ref.at[pl.ds(t, 1)], sem.at[1]).start(priority=1)
        return carry

    lax.fori_loop(0, TM, body, 0, unroll=8)
    pltpu.make_async_copy(ybuf_ref.at[pl.ds(0, TM)], ya_ref, sem.at[0]).wait()
    pltpu.make_async_copy(ybuf_ref.at[pl.ds(0, TM)], yb_ref, sem.at[1]).wait()
    m = mod_ref[0]
    y = ya_ref[...] + yb_ref[...]
    out = _layer_norm(alpha * x_ref[...] + (1.0 + m[5:6]) * y, lng_ref[...], lnb_ref[...])
    if split_blk is None:
        o_refs[0][...] = out
    else:
        @pl.when(pl.program_id(0) < split_blk)
        def _():
            o_refs[0][...] = out

        @pl.when(pl.program_id(0) >= split_blk)
        def _():
            o_refs[1][...] = out


def _combine(ybuf, dest, x, mod_l, ln_g, ln_b, segs, *, alpha, split=False):
    nt, d = x.shape
    const = lambda i: (0, 0)
    if split:
        n0 = segs[0][0] * segs[0][1]
        split_blk = n0 // TM
        out_specs = [pl.BlockSpec((TM, d), lambda i: (jnp.minimum(i, split_blk - 1), 0)),
                     pl.BlockSpec((TM, d), lambda i: (jnp.maximum(i - split_blk, 0), 0))]
        out_shape = [jax.ShapeDtypeStruct((n0, d), F32), jax.ShapeDtypeStruct((nt - n0, d), F32)]
    else:
        split_blk = None
        out_specs = [pl.BlockSpec((TM, d), lambda i: (i, 0))]
        out_shape = [jax.ShapeDtypeStruct((nt, d), F32)]
    res = pl.pallas_call(
        functools.partial(_combine_kernel, alpha=alpha, split_blk=split_blk),
        grid=(nt // TM,),
        in_specs=[pl.BlockSpec((2, TM), lambda i: (0, i), memory_space=pltpu.SMEM),
                  pl.BlockSpec(memory_space=pl.ANY),
                  pl.BlockSpec((TM, d), lambda i: (i, 0)),
                  pl.BlockSpec((1, 6, d), lambda i: (_mod_row(i, TM, segs), 0, 0)),
                  pl.BlockSpec((1, d), const), pl.BlockSpec((1, d), const)],
        out_specs=out_specs,
        out_shape=out_shape,
        scratch_shapes=[pltpu.VMEM((TM, d), F32), pltpu.VMEM((TM, d), F32), pltpu.SemaphoreType.DMA((2,))],
        compiler_params=_cparams(("arbitrary",)),
        name="moe_combine",
    )(dest, ybuf, x, mod_l, ln_g.reshape(1, d), ln_b.reshape(1, d))
    return res if split else res[0]


def _moe(x, mod_l, route, counts, ln_g, ln_b, wg, wu, wd, segs, *, alpha, split=False):
    nt = x.shape[0]
    cap = 2 * nt + N_EXPERTS * MOE_BLK
    n_blk = cap // MOE_BLK
    cnt = counts[:, 0].astype(jnp.int32)
    padded = (cnt + MOE_BLK - 1) // MOE_BLK * MOE_BLK
    pad_ends = jnp.cumsum(padded)
    pad_starts = (pad_ends - padded).astype(jnp.int32)
    blk_start = jnp.arange(n_blk, dtype=jnp.int32) * MOE_BLK
    blk_exp = jnp.minimum(jnp.sum(pad_ends[None, :] <= blk_start[:, None], axis=1), N_EXPERTS - 1).astype(jnp.int32)
    n_used = (pad_ends[-1:] // MOE_BLK).astype(jnp.int32)
    dest = _dest(route, pad_starts)
    xbuf = _dispatch(x, mod_l, route, dest, cap, segs)
    ybuf = _experts(xbuf, blk_exp, n_used, wg, wu, wd)
    return _combine(ybuf, dest, x, mod_l, ln_g, ln_b, segs, alpha=alpha, split=split)


def _attn_w_in_layout(w_in):
    scale = HEAD_DIM ** -0.5
    qa = w_in[:, 0:512] * scale
    ka = w_in[:, 512:640]
    va = w_in[:, 640:768]
    qb = w_in[:, 768:1280] * scale
    kb = w_in[:, 1280:1792]
    vb = w_in[:, 1792:2304]
    dup = lambda a: jnp.concatenate([a[:, 0:64], a[:, 0:64], a[:, 64:128], a[:, 64:128]], axis=1)
    return jnp.concatenate([qa, dup(ka), dup(va), qb, kb, vb], axis=1).astype(BF16)


def _rec_w_in_layout(w_in):
    d = D_MODEL
    return jnp.concatenate([w_in[:, d:3 * d], w_in[:, 0:d], w_in[:, 3 * d:5 * d]], axis=1).astype(BF16)


def _rope_tables(t_max):
    inv = jnp.power(ROPE_THETA, -jnp.arange(0, HEAD_DIM, 2, dtype=F32) / HEAD_DIM)
    ang = jnp.arange(t_max, dtype=F32)[:, None] * inv[None, :]
    cos = jnp.cos(ang)
    sin = jnp.sin(ang)
    cos128 = jnp.concatenate([cos, cos, cos, cos], axis=1)
    sin128 = jnp.concatenate([-sin, sin, -sin, sin], axis=1)
    return cos128, sin128


def kernel(x_prompt, x_sample, c_prompt, c_sample, ada_w, ada_b, ln_g, ln_b, attn_w_in, attn_sink, nat_rel_bias, attn_w_out, rec_w_in, rec_lb, rec_gnorm, rec_w_out, router_w_group, router_b_group, router_w_expert, router_b_expert, expert_w_gate, expert_w_up, expert_w_down):
    depth = ada_w.shape[0]
    d = x_prompt.shape[-1]
    segs = ((x_prompt.shape[0], x_prompt.shape[1]), (x_sample.shape[0], x_sample.shape[1]))
    alpha = (2 * depth) ** 0.25
    x = jnp.concatenate([x_prompt.reshape(-1, d), x_sample.reshape(-1, d)], axis=0)
    c = jnp.concatenate([c_prompt, c_sample], axis=0)
    nseq = c.shape[0]
    mod = _adaln(c, ada_w, ada_b).reshape(depth, nseq, 6, d)
    rope_tables = _rope_tables(max(t for _, t in segs))
    for l in range(depth):
        mod_l = mod[l]
        i = l // 2
        router = _router_tables(router_w_group[l], router_b_group[l], router_w_expert[l], router_b_expert[l])
        if l % 2 == 0:
            proj = _inproj_attn(x, mod_l, _attn_w_in_layout(attn_w_in[i]), segs, rope_tables)
            oa = _win_attn(proj, attn_sink[i].astype(F32), segs)
            ob = _nat_attn(proj, _nat_bias_table(nat_rel_bias[i]), segs)
            x, route, counts = _mix_out((oa, ob), attn_w_out[i].astype(BF16), x, mod_l, ln_g[l, 0], ln_b[l, 0],
                                        router, segs, alpha=alpha)
        else:
            proj_z, proj_qvg = _inproj_rec(x, mod_l, _rec_w_in_layout(rec_w_in[i]), segs)
            o_f, o_b = _hgrn(proj_z, proj_qvg, rec_lb.astype(F32), i, segs)
            x, route, counts = _mix_out((o_f, o_b), rec_w_out[i].astype(BF16), x, mod_l, ln_g[l, 0], ln_b[l, 0],
                                        router, segs, alpha=alpha, rec_extra=(proj_qvg, rec_gnorm[i].astype(F32)))
        x = _moe(x, mod_l, route, counts, ln_g[l, 1], ln_b[l, 1], expert_w_gate[l], expert_w_up[l],
                 expert_w_down[l], segs, alpha=alpha, split=(l == depth - 1))
    return (x[0].reshape(x_prompt.shape), x[1].reshape(x_sample.shape))
```

```python
import functools

import numpy as np
import jax
import jax.numpy as jnp
from jax import lax
from jax.experimental import pallas as pl
from jax.experimental.pallas import tpu as pltpu

F32 = jnp.float32
BF16 = jnp.bfloat16
NEG_INF = float("-inf")

D_MODEL = 1024
HEAD_DIM = 64
A_HEADS = 8
A_KV_HEADS = 2
A_WINDOW = 128
A_BLOCK = 128
B_HEADS = 8
GRID_W = 64
NB_ROWS = 8
NB_COLS = 16
ROPE_THETA = 10000.0
C_HEADS = 8
C_KEY_DIM = 128
C_CHUNK = 64
N_GROUPS = 4
EXPERTS_PER_GROUP = 8
N_EXPERTS = 32
D_EXPERT = 512
LN_EPS = 1e-5
RMS_EPS = 1e-6
LOG2E = 1.4426950408889634

LANES = 128
ATTN_W = 2560
ROPE_W = 768
REC_W = 5 * D_MODEL
ROUTER_ROWS = 40
ROUTE_ROWS = 16
ROW_TILE = D_MODEL // LANES

TM = 512
TM_IN = 1024
MOE_BLK = 256
NAT_BLK = 512
REC_BLK = 512
VMEM_LIMIT = 48 * 1024 * 1024


def _cparams(sem):
    return pltpu.CompilerParams(dimension_semantics=sem, vmem_limit_bytes=VMEM_LIMIT)


def _decode(i, blk, segs):
    out = None
    blk0 = 0
    for (b, t) in segs:
        nb = t // blk
        j = (i - blk0) % nb
        cand = (i - j, j, nb)
        out = cand if out is None else tuple(jnp.where(i < blk0, o, c) for o, c in zip(out, cand))
        blk0 += b * nb
    return out


def _mod_row(i, blk, segs):
    out = None
    blk0 = 0
    row0 = 0
    for (b, t) in segs:
        nb = t // blk
        cand = row0 + (i - blk0) // nb
        out = cand if out is None else jnp.where(i < blk0, out, cand)
        blk0 += b * nb
        row0 += b
    return out


def _pos_block(i, blk, segs):
    return _decode(i, blk, segs)[1]


def _dot(a, b):
    return jnp.dot(a, b, preferred_element_type=F32)


def _dot_nt(a, b):
    return lax.dot_general(a, b, (((1,), (1,)), ((), ())), preferred_element_type=F32)


def _dot_tn(a, b):
    return lax.dot_general(a, b, (((0,), (0,)), ((), ())), preferred_element_type=F32)


def _split2(x):
    hi = x.astype(BF16)
    lo = (x - hi.astype(F32)).astype(BF16)
    return hi, lo


def _layer_norm(x, g, b):
    mu = jnp.mean(x, axis=-1, keepdims=True)
    xc = x - mu
    var = jnp.mean(xc * xc, axis=-1, keepdims=True)
    return xc * lax.rsqrt(var + LN_EPS) * g + b


def _adaln_kernel(c_ref, w_ref, b_ref, o_ref):
    c = c_ref[...]
    cs = c * jax.nn.sigmoid(c)
    c_hi, c_lo = _split2(cs)
    w_hi, w_lo = _split2(w_ref[0])
    o_ref[0] = _dot(c_hi, w_hi) + _dot(c_hi, w_lo) + _dot(c_lo, w_hi) + b_ref[0]


def _adaln(c, ada_w, ada_b):
    depth, d, n = ada_w.shape
    nb = c.shape[0]
    tn = 1536
    return pl.pallas_call(
        _adaln_kernel,
        grid=(depth, n // tn),
        in_specs=[
            pl.BlockSpec((nb, d), lambda l, j: (0, 0)),
            pl.BlockSpec((1, d, tn), lambda l, j: (l, 0, j)),
            pl.BlockSpec((1, 1, tn), lambda l, j: (l, 0, j)),
        ],
        out_specs=pl.BlockSpec((1, nb, tn), lambda l, j: (l, 0, j)),
        out_shape=jax.ShapeDtypeStruct((depth, nb, n), F32),
        compiler_params=_cparams(("parallel", "parallel")),
        name="adaln",
    )(c, ada_w, ada_b.reshape(depth, 1, n))


def _rot_half_pairs(x):
    lane = lax.broadcasted_iota(jnp.int32, x.shape, 1)
    first = (lane & 63) < 32
    return jnp.where(first, pltpu.roll(x, 96, 1), pltpu.roll(x, 32, 1))


def _modulate_once(x_ref, mod_ref, xb_ref):
    @pl.when(pl.program_id(1) == 0)
    def _():
        m = mod_ref[0]
        xb_ref[...] = (x_ref[...] * (1.0 + m[1:2]) + m[0:1]).astype(BF16)


def _inproj_attn_kernel(x_ref, mod_ref, w_ref, cos_ref, sin_ref, o_ref, xb_ref):
    _modulate_once(x_ref, mod_ref, xb_ref)
    y = _dot(xb_ref[...], w_ref[...])

    @pl.when(pl.program_id(1) == 0)
    def _():
        cos = cos_ref[...]
        sin = sin_ref[...]
        for g in range(ROPE_W // LANES):
            blk = y[:, g * LANES:(g + 1) * LANES]
            o_ref[:, g * LANES:(g + 1) * LANES] = (blk * cos + _rot_half_pairs(blk) * sin).astype(o_ref.dtype)
        o_ref[:, ROPE_W:] = y[:, ROPE_W:].astype(o_ref.dtype)

    @pl.when(pl.program_id(1) != 0)
    def _():
        o_ref[...] = y.astype(o_ref.dtype)


def _inproj_rec_kernel(x_ref, mod_ref, w_ref, oz_ref, oq_ref, xb_ref, *, nz):
    _modulate_once(x_ref, mod_ref, xb_ref)
    y = _dot(xb_ref[...], w_ref[...])

    @pl.when(pl.program_id(1) < nz)
    def _():
        oz_ref[...] = y

    @pl.when(pl.program_id(1) >= nz)
    def _():
        oq_ref[...] = y.astype(oq_ref.dtype)


def _inproj_specs(x, segs, tn):
    d = x.shape[1]
    return [pl.BlockSpec((TM_IN, d), lambda i, j: (i, 0)),
            pl.BlockSpec((1, 6, d), lambda i, j: (_mod_row(i, TM_IN, segs), 0, 0)),
            pl.BlockSpec((d, tn), lambda i, j: (0, j))]


def _inproj_attn(x, mod_l, w, segs, rope_tables):
    nt, d = x.shape
    n = w.shape[1]
    tn = n // 2
    tab = pl.BlockSpec((TM_IN, LANES), lambda i, j: (_pos_block(i, TM_IN, segs), 0))
    return pl.pallas_call(
        _inproj_attn_kernel,
        grid=(nt // TM_IN, n // tn),
        in_specs=_inproj_specs(x, segs, tn) + [tab, tab],
        out_specs=pl.BlockSpec((TM_IN, tn), lambda i, j: (i, j)),
        out_shape=jax.ShapeDtypeStruct((nt, n), BF16),
        scratch_shapes=[pltpu.VMEM((TM_IN, d), BF16)],
        compiler_params=_cparams(("parallel", "arbitrary")),
        name="inproj_rope",
    )(x, mod_l, w, *rope_tables)


def _inproj_rec(x, mod_l, w, segs):
    nt, d = x.shape
    tn = D_MODEL
    nz = 2
    nq = w.shape[1] // tn - nz
    return pl.pallas_call(
        functools.partial(_inproj_rec_kernel, nz=nz),
        grid=(nt // TM_IN, nz + nq),
        in_specs=_inproj_specs(x, segs, tn),
        out_specs=[pl.BlockSpec((TM_IN, tn), lambda i, j: (i, jnp.minimum(j, nz - 1))),
                   pl.BlockSpec((TM_IN, tn), lambda i, j: (i, jnp.maximum(j - nz, 0)))],
        out_shape=[jax.ShapeDtypeStruct((nt, nz * tn), F32), jax.ShapeDtypeStruct((nt, nq * tn), BF16)],
        scratch_shapes=[pltpu.VMEM((TM_IN, d), BF16)],
        compiler_params=_cparams(("parallel", "arbitrary")),
        name="inproj_rec",
    )(x, mod_l, w)


def _win_attn_kernel(sink_ref, q_ref, k0_ref, k1_ref, k2_ref, v0_ref, v1_ref, v2_ref, o_ref, *, segs):
    i = pl.program_id(0)
    _, j, nb = _decode(i, A_BLOCK, segs)
    ws = jnp.clip(j - 1, 0, nb - 3)
    span = 3 * A_BLOCK
    qpos = j * A_BLOCK + lax.broadcasted_iota(jnp.int32, (A_BLOCK, span), 0)
    kpos = ws * A_BLOCK + lax.broadcasted_iota(jnp.int32, (A_BLOCK, span), 1)
    valid = jnp.abs(qpos - kpos) <= A_WINDOW
    lo = lax.broadcasted_iota(jnp.int32, (A_BLOCK, LANES), 1) < HEAD_DIM
    kfull = jnp.concatenate([k0_ref[...], k1_ref[...], k2_ref[...]], axis=0)
    vfull = jnp.concatenate([v0_ref[...], v1_ref[...], v2_ref[...]], axis=0)
    zero = jnp.zeros((A_BLOCK, LANES), BF16)
    grp = A_HEADS // A_KV_HEADS
    for kvh in range(A_KV_HEADS):
        kk = kfull[:, kvh * LANES:(kvh + 1) * LANES]
        vv = vfull[:, kvh * LANES:(kvh + 1) * LANES]
        parts = []
        for p in range(grp // 2):
            c0 = (kvh * (grp // 2) + p) * LANES
            qp = q_ref[:, c0:c0 + LANES]
            parts.append(jnp.where(lo, qp, zero))
            parts.append(jnp.where(lo, zero, qp))
        qs = jnp.concatenate(parts, axis=0)
        s = _dot_nt(qs, kk)
        probs = []
        inv = []
        for h in range(grp):
            sh = jnp.where(valid, s[h * A_BLOCK:(h + 1) * A_BLOCK], NEG_INF)
            sink = sink_ref[kvh * grp + h]
            m = jnp.maximum(jnp.max(sh, axis=-1, keepdims=True), sink)
            e = jnp.exp(sh - m)
            den = jnp.sum(e, axis=-1, keepdims=True) + jnp.exp(sink - m)
            probs.append(e.astype(BF16))
            inv.append(1.0 / den)
        o = _dot(jnp.concatenate(probs, axis=0), vv)
        for p in range(grp // 2):
            oe = o[(2 * p) * A_BLOCK:(2 * p + 1) * A_BLOCK] * inv[2 * p]
            oo = o[(2 * p + 1) * A_BLOCK:(2 * p + 2) * A_BLOCK] * inv[2 * p + 1]
            c0 = (kvh * (grp // 2) + p) * LANES
            o_ref[:, c0:c0 + LANES] = jnp.where(lo, oe, oo).astype(o_ref.dtype)


def _win_attn(proj, sink, segs):
    nt = proj.shape[0]
    kvw = 2 * LANES

    def kv_map(d, col):
        def f(i):
            base, j, nb = _decode(i, A_BLOCK, segs)
            return (base + jnp.clip(j - 1, 0, nb - 3) + d, col)
        return f

    in_specs = [pl.BlockSpec(memory_space=pltpu.SMEM),
                pl.BlockSpec((A_BLOCK, 512), lambda i: (i, 0))]
    in_specs += [pl.BlockSpec((A_BLOCK, kvw), kv_map(d, 2)) for d in range(3)]
    in_specs += [pl.BlockSpec((A_BLOCK, kvw), kv_map(d, 3)) for d in range(3)]
    return pl.pallas_call(
        functools.partial(_win_attn_kernel, segs=segs),
        grid=(nt // A_BLOCK,),
        in_specs=in_specs,
        out_specs=pl.BlockSpec((A_BLOCK, 512), lambda i: (i, 0)),
        out_shape=jax.ShapeDtypeStruct((nt, 512), BF16),
        compiler_params=_cparams(("parallel",)),
        name="win_attn",
    )(sink, proj, proj, proj, proj, proj, proj, proj)


def _nat_bias_table(rel_bias):
    c = np.arange(GRID_W)
    kc = np.arange(GRID_W)
    qwin = np.clip(c - NB_COLS // 2, 0, GRID_W - NB_COLS)
    ok = (kc[None, :] >= qwin[:, None]) & (kc[None, :] < qwin[:, None] + NB_COLS)
    dc = np.clip(kc[None, :] - c[:, None], -(NB_COLS - 1), NB_COLS - 1) + NB_COLS - 1
    var = np.arange(NB_ROWS)
    irow = np.arange(NB_ROWS)
    dr = irow[None, :] - var[:, None] + NB_ROWS - 1
    t = rel_bias.astype(F32)[:, dr][:, :, :, dc]
    t = jnp.where(jnp.asarray(ok)[None, None, None], t, NEG_INF)
    t = t.transpose(1, 0, 3, 2, 4)
    return t.reshape(NB_ROWS, B_HEADS, GRID_W, NB_ROWS * GRID_W)


def _nat_attn_kernel(q_ref, k0, k1, k2, k3, v0, v1, v2, v3, bias_ref, o_ref, kbuf, vbuf, *, segs):
    i = pl.program_id(0)
    _, j, nb = _decode(i, NAT_BLK, segs)
    rows = nb * NB_ROWS
    w0 = jnp.clip(NB_ROWS * j - NB_ROWS // 2, 0, rows - 2 * NB_ROWS)
    for d, (kr, vr) in enumerate(((k0, v0), (k1, v1), (k2, v2), (k3, v3))):
        kbuf[d * 256:(d + 1) * 256, :] = kr[...]
        vbuf[d * 256:(d + 1) * 256, :] = vr[...]
    lo = lax.broadcasted_iota(jnp.int32, (GRID_W, LANES), 1) < HEAD_DIM
    zero = jnp.zeros((GRID_W, LANES), BF16)
    nkeys = NB_ROWS * GRID_W
    for rr in range(NB_ROWS):
        r = NB_ROWS * j + rr
        rs = jnp.clip(r - NB_ROWS // 2, 0, rows - NB_ROWS)
        var = r - rs
        off = pl.multiple_of((rs - w0) * GRID_W, GRID_W)
        for hp in range(B_HEADS // 2):
            cs = slice(hp * LANES, (hp + 1) * LANES)
            kp = kbuf[pl.ds(off, nkeys), cs]
            vp = vbuf[pl.ds(off, nkeys), cs]
            qp = q_ref[rr * GRID_W:(rr + 1) * GRID_W, cs]
            qs = jnp.concatenate([jnp.where(lo, qp, zero), jnp.where(lo, zero, qp)], axis=0)
            s = _dot_nt(qs, kp)
            bias = jnp.concatenate([bias_ref[var, 2 * hp], bias_ref[var, 2 * hp + 1]], axis=0)
            s = s + bias
            m = jnp.max(s, axis=-1, keepdims=True)
            e = jnp.exp(s - m)
            inv = 1.0 / jnp.sum(e, axis=-1, keepdims=True)
            o = _dot(e.astype(BF16), vp) * inv
            o_ref[rr * GRID_W:(rr + 1) * GRID_W, cs] = jnp.where(lo, o[:GRID_W], o[GRID_W:]).astype(o_ref.dtype)


def _nat_attn(proj, bias_table, segs):
    nt = proj.shape[0]

    def kv_map(d, col):
        def f(i):
            base, j, nb = _decode(i, NAT_BLK, segs)
            return (2 * base + jnp.clip(2 * j - 1, 0, 2 * nb - 4) + d, col)
        return f

    in_specs = [pl.BlockSpec((NAT_BLK, 512), lambda i: (i, 2))]
    in_specs += [pl.BlockSpec((256, 512), kv_map(d, 3)) for d in range(4)]
    in_specs += [pl.BlockSpec((256, 512), kv_map(d, 4)) for d in range(4)]
    in_specs += [pl.BlockSpec(bias_table.shape, lambda i: (0, 0, 0, 0))]
    return pl.pallas_call(
        functools.partial(_nat_attn_kernel, segs=segs),
        grid=(nt // NAT_BLK,),
        in_specs=in_specs,
        out_specs=pl.BlockSpec((NAT_BLK, 512), lambda i: (i, 0)),
        out_shape=jax.ShapeDtypeStruct((nt, 512), BF16),
        scratch_shapes=[pltpu.VMEM((1024, 512), BF16), pltpu.VMEM((1024, 512), BF16)],
        compiler_params=_cparams(("parallel",)),
        name="nat_attn",
    )(*([proj] * 9), bias_table)


def _chunk_cumsum(x, fwd):
    n = x.shape[0]
    x3 = x.reshape(n // 8, 8, LANES)
    sub = lax.broadcasted_iota(jnp.int32, x3.shape, 1)
    s = 1
    while s < 8:
        if fwd:
            x3 = x3 + jnp.where(sub >= s, pltpu.roll(x3, s, 1), 0.0)
        else:
            x3 = x3 + jnp.where(sub < 8 - s, pltpu.roll(x3, 8 - s, 1), 0.0)
        s *= 2
    nv = n // 8
    outs = [None] * nv
    edges = [None] * nv
    run = None
    edge = 7 if fwd else 0
    for j in (range(nv) if fwd else range(nv - 1, -1, -1)):
        cur = x3[j]
        if run is not None:
            cur = cur + run
        outs[j] = cur
        run = jnp.broadcast_to(cur[edge:edge + 1, :], (8, LANES))
        edges[j] = run
    return outs, edges


def _split_rows(outs, edges, m, fwd):
    nv = len(outs)
    if m >= 8:
        g = m // 8
        res = []
        for j in range(nv):
            first = (j // (2 * g)) * 2 * g
            res.append(edges[first + g - 1] if fwd else edges[first + g])
        return res
    sub = lax.broadcasted_iota(jnp.int32, (8, LANES), 0)
    pick = m - 1 if fwd else m
    res = []
    for j in range(nv):
        r = None
        for gi in reversed(range(8 // (2 * m))):
            row = gi * 2 * m + pick
            bc = jnp.broadcast_to(outs[j][row:row + 1, :], (8, LANES))
            r = bc if r is None else jnp.where(sub < (gi + 1) * 2 * m, bc, r)
        res.append(r)
    return res


def _hgrn_chunk(q, z, vb, lower, masks, fwd):
    n = C_CHUNK
    f = lower + (1.0 - lower) * jax.nn.sigmoid(z)
    kk = 1.0 - f
    outs, edges = _chunk_cumsum(jnp.log2(f), fwd)
    b = jnp.concatenate(outs, axis=0)
    total = b[n - 1:n] if fwd else b[0:1]
    rowi = lax.broadcasted_iota(jnp.int32, (n, 1), 0)
    zero = jnp.zeros((n, LANES), BF16)
    xs = []
    m = n // 2
    while m >= 2:
        r = jnp.concatenate(_split_rows(outs, edges, m, fwd), axis=0)
        upper = (rowi & (2 * m - 1)) >= m
        qside = upper if fwd else jnp.logical_not(upper)
        e = jnp.exp2(jnp.where(qside, b - r, r - b))
        xs.append((jnp.where(qside, q, kk) * e).astype(BF16))
        m //= 2
    odd = (rowi & 1) == 1
    qside = odd if fwd else jnp.logical_not(odd)
    xs.append(jnp.where(qside, q * f, kk).astype(BF16))
    ti = lax.broadcasted_iota(jnp.int32, (n, LANES), 0)
    li = lax.broadcasted_iota(jnp.int32, (n, LANES), 1)
    asum = jnp.where(ti == li, jnp.sum(q * kk, axis=-1, keepdims=True), 0.0)
    for p in range(len(xs) // 2):
        x1, x2 = xs[2 * p], xs[2 * p + 1]
        lhs = jnp.concatenate([x1, x2], axis=1)
        rhs = jnp.concatenate([jnp.concatenate([x1, zero], axis=1), jnp.concatenate([zero, x2], axis=1)], axis=0)
        asum = asum + _dot_nt(lhs, rhs) * masks[p]
    ke = (kk * jnp.exp2(total - b)).astype(BF16)
    u = _dot_tn(vb, ke)
    o = _dot(asum.astype(BF16), jnp.concatenate([vb, vb], axis=0))
    qe = (q * jnp.exp2(b)).astype(BF16)
    return o, qe, u, jnp.exp2(total)


def _level_masks(fwd):
    n = C_CHUNK
    t = np.arange(n)[:, None]
    s = np.arange(n)[None, :]
    out = np.zeros((3, n, 2 * n), np.float32)
    for i, m in enumerate([32, 16, 8, 4, 2, 1]):
        same = (t // (2 * m)) == (s // (2 * m))
        tu = (t % (2 * m)) >= m
        su = (s % (2 * m)) >= m
        ok = same & ((tu & ~su) if fwd else (~tu & su))
        out[i // 2, :, (i % 2) * n:(i % 2 + 1) * n] = ok
    return out


def _hgrn_kernel(lb_ref, mf_ref, mb_ref, qf_ref, zf_ref, vf_ref, qb_ref, zb_ref, vb_ref,
                 of_ref, ob_ref, sf_ref, sb_ref, *, segs, layer):
    i = pl.program_id(1)
    _, j, _ = _decode(i, REC_BLK, segs)

    @pl.when(j == 0)
    def _():
        sf_ref[...] = jnp.zeros_like(sf_ref)
        sb_ref[...] = jnp.zeros_like(sb_ref)

    lb = lb_ref[...]
    ex = jnp.exp(lb - jnp.max(lb, axis=0, keepdims=True))
    p = ex / jnp.sum(ex, axis=0, keepdims=True)
    lower = jnp.sum(p[0:layer + 1], axis=0, keepdims=True) - p[0:1]
    mf = [mf_ref[k] for k in range(3)]
    mb = [mb_ref[k] for k in range(3)]
    nch = REC_BLK // C_CHUNK
    dirs = ((qf_ref, zf_ref, vf_ref, of_ref, sf_ref, mf, True), (qb_ref, zb_ref, vb_ref, ob_ref, sb_ref, mb, False))
    parts = ([], [])
    for c in range(nch):
        for d, (q_ref, z_ref, v_ref, _, _, masks, fwd) in enumerate(dirs):
            cc = c if fwd else nch - 1 - c
            sl = slice(cc * C_CHUNK, (cc + 1) * C_CHUNK)
            parts[d].append((sl,) + _hgrn_chunk(q_ref[sl, :].astype(F32), z_ref[sl, :], v_ref[sl, :], lower, masks, fwd))
    for d, (_, _, _, o_ref, s_ref, _, _) in enumerate(dirs):
        st = s_ref[...]
        for (sl, o, qe, u, dec) in parts[d]:
            o_ref[sl, :] = o + _dot_nt(qe, st.astype(BF16))
            st = st * dec + u
        s_ref[...] = st


def _hgrn(proj_z, proj_qvg, rec_lb, layer, segs):
    nt = proj_z.shape[0]
    nrec = rec_lb.shape[0]
    mf = jnp.asarray(_level_masks(True))
    mb = jnp.asarray(_level_masks(False))

    def fwd_map(col):
        return lambda h, i: (i, col * C_HEADS + h)

    def bwd_map(col):
        def f(h, i):
            base, j, nb = _decode(i, REC_BLK, segs)
            return (base + nb - 1 - j, col * C_HEADS + h)
        return f

    blk = (REC_BLK, C_KEY_DIM)
    c3 = lambda h, i: (0, 0, 0)
    in_specs = [pl.BlockSpec((nrec, C_KEY_DIM), lambda h, i: (0, h)),
                pl.BlockSpec(mf.shape, c3), pl.BlockSpec(mb.shape, c3),
                pl.BlockSpec(blk, fwd_map(0)), pl.BlockSpec(blk, fwd_map(0)), pl.BlockSpec(blk, fwd_map(1)),
                pl.BlockSpec(blk, bwd_map(0)), pl.BlockSpec(blk, bwd_map(1)), pl.BlockSpec(blk, bwd_map(1))]
    out_shape = [jax.ShapeDtypeStruct((nt, D_MODEL), F32)] * 2
    return pl.pallas_call(
        functools.partial(_hgrn_kernel, segs=segs, layer=layer),
        grid=(C_HEADS, nt // REC_BLK),
        in_specs=in_specs,
        out_specs=[pl.BlockSpec(blk, fwd_map(0)), pl.BlockSpec(blk, bwd_map(0))],
        out_shape=out_shape,
        scratch_shapes=[pltpu.VMEM((C_KEY_DIM, C_KEY_DIM), F32)] * 2,
        compiler_params=_cparams(("parallel", "arbitrary")),
        name="hgrn",
    )(rec_lb, mf, mb, proj_qvg, proj_z, proj_qvg, proj_qvg, proj_z, proj_qvg)


def _route_rows(logits_t, utri, cnt_ref):
    tm = logits_t.shape[1]
    gi = lax.broadcasted_iota(jnp.int32, (8, tm), 0).astype(F32)
    lg = jnp.where(gi < N_GROUPS, logits_t[0:8], NEG_INF)
    gm = jnp.max(lg, axis=0, keepdims=True)
    p_grp = 1.0 / jnp.sum(jnp.exp(lg - gm), axis=0, keepdims=True)
    grp = jnp.min(jnp.where(lg == gm, gi, 99.0), axis=0, keepdims=True)
    ei = lax.broadcasted_iota(jnp.int32, (N_EXPERTS, tm), 0).astype(F32)
    egrp = (lax.broadcasted_iota(jnp.int32, (N_EXPERTS, tm), 0) >> 3).astype(F32)
    le = jnp.where(egrp == grp, logits_t[8:8 + N_EXPERTS], NEG_INF)
    v1 = jnp.max(le, axis=0, keepdims=True)
    i1 = jnp.min(jnp.where(le == v1, ei, 99.0), axis=0, keepdims=True)
    le2 = jnp.where(ei == i1, NEG_INF, le)
    v2 = jnp.max(le2, axis=0, keepdims=True)
    i2 = jnp.min(jnp.where(le2 == v2, ei, 99.0), axis=0, keepdims=True)
    t = jnp.exp(v2 - v1)
    g1 = p_grp / (1.0 + t)
    g2 = p_grp * t / (1.0 + t)
    o1 = jnp.where(ei == i1, 1.0, 0.0)
    o2 = jnp.where(ei == i2, 1.0, 0.0)
    osum = o1 + o2
    before = _dot(osum.astype(BF16), utri) + cnt_ref[:, 0:1]
    rank1 = jnp.sum(o1 * before, axis=0, keepdims=True)
    rank2 = jnp.sum(o2 * before, axis=0, keepdims=True)
    cnt_ref[...] = cnt_ref[...] + jnp.sum(osum, axis=1, keepdims=True)

    def pieces(g):
        hi = g.astype(BF16).astype(F32)
        r = g - hi
        mid = r.astype(BF16).astype(F32)
        return [hi, mid, r - mid]

    rows = [i1, i2, rank1, rank2] + pieces(g1) + pieces(g2)
    return rows + [jnp.zeros((1, tm), F32)] * (ROUTE_ROWS - len(rows))


def _mix_out_kernel(*refs, rec, alpha):
    if rec:
        (a_ref, b_ref, g_ref, gn_ref, w_ref, x_ref, mod_ref, lng_ref, lnb_ref,
         rw_hi_ref, rw_lo_ref, rb_ref, utri_ref, xo_ref, route_ref, cnt_out_ref, cnt_ref) = refs
    else:
        (a_ref, b_ref, w_ref, x_ref, mod_ref, lng_ref, lnb_ref,
         rw_hi_ref, rw_lo_ref, rb_ref, utri_ref, xo_ref, route_ref, cnt_out_ref, cnt_ref) = refs
    i = pl.program_id(0)

    @pl.when(i == 0)
    def _():
        cnt_ref[...] = jnp.zeros_like(cnt_ref)

    if rec:
        o = a_ref[...] + b_ref[...]
        g = g_ref[...].astype(F32)
        gate = gn_ref[...] * (g * jax.nn.sigmoid(g))
        parts = []
        for h in range(C_HEADS):
            blk = o[:, h * C_KEY_DIM:(h + 1) * C_KEY_DIM]
            ms = jnp.mean(blk * blk, axis=-1, keepdims=True)
            parts.append(blk * lax.rsqrt(ms + RMS_EPS))
        lhs = (jnp.concatenate(parts, axis=1) * gate).astype(BF16)
        y = _dot(lhs, w_ref[...])
    else:
        half = a_ref.shape[1]
        y = _dot(a_ref[...], w_ref[0:half, :]) + _dot(b_ref[...], w_ref[half:, :])
    m = mod_ref[0]
    xn = _layer_norm(alpha * x_ref[...] + (1.0 + m[2:3]) * y, lng_ref[...], lnb_ref[...])
    xo_ref[...] = xn
    h_hi, h_lo = _split2(xn * (1.0 + m[4:5]) + m[3:4])
    rw_hi = rw_hi_ref[...]
    logits_t = _dot_nt(rw_hi, h_hi) + _dot_nt(rw_hi, h_lo) + _dot_nt(rw_lo_ref[...], h_hi) + rb_ref[:, 0:1]
    for k, row in enumerate(_route_rows(logits_t, utri_ref[...], cnt_ref)):
        route_ref[k:k + 1, :] = row
    cnt_out_ref[...] = cnt_ref[...]


def _router_tables(w_rg, b_rg, w_re, b_re):
    d = w_rg.shape[0]
    wt = jnp.zeros((ROUTER_ROWS, d), F32)
    wt = wt.at[0:N_GROUPS].set(w_rg.astype(F32).T).at[8:8 + N_EXPERTS].set(w_re.astype(F32).T)
    hi = wt.astype(BF16)
    lo = (wt - hi.astype(F32)).astype(BF16)
    rb = jnp.zeros((ROUTER_ROWS,), F32).at[0:N_GROUPS].set(b_rg.astype(F32))
    rb = rb.at[8:8 + N_EXPERTS].set(b_re.astype(F32).reshape(-1))
    return hi, lo, jnp.broadcast_to(rb[:, None], (ROUTER_ROWS, LANES))


def _mix_out(lhs, w_out, x, mod_l, ln_g, ln_b, router, segs, *, alpha, rec_extra=None):
    nt, d = x.shape
    rec = rec_extra is not None
    rw_hi, rw_lo, rb = router
    idx = np.arange(TM)
    utri = jnp.asarray(idx[:, None] < idx[None, :], BF16)
    row = lambda i: (i, 0)
    const = lambda i: (0, 0)
    a, b = lhs
    in_specs = [pl.BlockSpec((TM, a.shape[1]), row), pl.BlockSpec((TM, b.shape[1]), row)]
    args = [a, b]
    if rec:
        proj, gnorm = rec_extra
        in_specs += [pl.BlockSpec((TM, d), lambda i: (i, 2)), pl.BlockSpec((1, d), const)]
        args += [proj, gnorm.reshape(1, d)]
    in_specs += [pl.BlockSpec(w_out.shape, const), pl.BlockSpec((TM, d), row),
                 pl.BlockSpec((1, 6, d), lambda i: (_mod_row(i, TM, segs), 0, 0)),
                 pl.BlockSpec((1, d), const), pl.BlockSpec((1, d), const),
                 pl.BlockSpec(rw_hi.shape, const), pl.BlockSpec(rw_lo.shape, const),
                 pl.BlockSpec(rb.shape, const), pl.BlockSpec((TM, TM), const)]
    args += [w_out, x, mod_l, ln_g.reshape(1, d), ln_b.reshape(1, d), rw_hi, rw_lo, rb, utri]
    return pl.pallas_call(
        functools.partial(_mix_out_kernel, rec=rec, alpha=alpha),
        grid=(nt // TM,),
        in_specs=in_specs,
        out_specs=[pl.BlockSpec((TM, d), row), pl.BlockSpec((ROUTE_ROWS, TM), lambda i: (0, i)),
                   pl.BlockSpec((N_EXPERTS, LANES), const)],
        out_shape=[jax.ShapeDtypeStruct((nt, d), F32), jax.ShapeDtypeStruct((ROUTE_ROWS, nt), F32),
                   jax.ShapeDtypeStruct((N_EXPERTS, LANES), F32)],
        scratch_shapes=[pltpu.VMEM((N_EXPERTS, LANES), F32)],
        compiler_params=_cparams(("arbitrary",)),
        name="mix_out_rec" if rec else "mix_out_attn",
    )(*args)


def _rows_to_tiles(ref, x):
    n = x.shape[0]
    for s in range(ROW_TILE):
        ref[pl.ds(s, n, stride=ROW_TILE), :] = x[:, s * LANES:(s + 1) * LANES]


def _tiles_to_rows(ref, n):
    return jnp.concatenate([ref[pl.ds(s, n, stride=ROW_TILE), :] for s in range(ROW_TILE)], axis=1)


def _dest_kernel(ps_ref, route_ref, o0_ref, o1_ref):
    r = route_ref[...]
    for k, o_ref in enumerate((o0_ref, o1_ref)):
        eid = r[k:k + 1].astype(jnp.int32)
        acc = jnp.zeros_like(eid)
        for e in range(N_EXPERTS):
            acc = jnp.where(eid == e, ps_ref[e], acc)
        o_ref[...] = (acc + r[2 + k:3 + k].astype(jnp.int32)) * ROW_TILE


def _dest(route, pad_starts):
    nt = route.shape[1]
    tb = 2048
    return pl.pallas_call(
        _dest_kernel,
        grid_spec=pltpu.PrefetchScalarGridSpec(
            num_scalar_prefetch=1,
            grid=(nt // tb,),
            in_specs=[pl.BlockSpec((ROUTE_ROWS, tb), lambda i, ps: (0, i))],
            out_specs=[pl.BlockSpec((1, tb), lambda i, ps: (0, i))] * 2,
        ),
        out_shape=[jax.ShapeDtypeStruct((1, nt), jnp.int32)] * 2,
        compiler_params=_cparams(("parallel",)),
        name="moe_dest",
    )(pad_starts, route)


def _dispatch_kernel(d0_ref, d1_ref, x_ref, mod_ref, xbuf_in_ref, xbuf_ref, h_ref, sem):
    del xbuf_in_ref
    m = mod_ref[0]
    _rows_to_tiles(h_ref, x_ref[...] * (1.0 + m[4:5]) + m[3:4])

    def body(t, carry):
        src = h_ref.at[pl.ds(pl.multiple_of(t * ROW_TILE, ROW_TILE), ROW_TILE)]
        for k, d_ref in enumerate((d0_ref, d1_ref)):
            dst = xbuf_ref.at[pl.ds(pl.multiple_of(d_ref[0, t], ROW_TILE), ROW_TILE)]
            pltpu.make_async_copy(src, dst, sem.at[k]).start(priority=k)
        return carry

    lax.fori_loop(0, TM, body, 0, unroll=8)
    for k in range(2):
        pltpu.make_async_copy(h_ref, xbuf_ref.at[pl.ds(0, TM * ROW_TILE)], sem.at[k]).wait()


def _dispatch(x, mod_l, dest, xbuf, segs):
    nt, d = x.shape
    smem_row = pl.BlockSpec((1, TM), lambda i: (0, i), memory_space=pltpu.SMEM)
    return pl.pallas_call(
        _dispatch_kernel,
        grid=(nt // TM,),
        in_specs=[smem_row, smem_row,
                  pl.BlockSpec((TM, d), lambda i: (i, 0)),
                  pl.BlockSpec((1, 6, d), lambda i: (_mod_row(i, TM, segs), 0, 0)),
                  pl.BlockSpec(memory_space=pl.ANY)],
        out_specs=pl.BlockSpec(memory_space=pl.ANY),
        out_shape=jax.ShapeDtypeStruct(xbuf.shape, F32),
        scratch_shapes=[pltpu.VMEM((TM * ROW_TILE, LANES), F32), pltpu.SemaphoreType.DMA((2,))],
        input_output_aliases={4: 0},
        compiler_params=_cparams(("arbitrary",)),
        name="moe_dispatch",
    )(dest[0], dest[1], x, mod_l, xbuf)


def _expert_kernel(be_ref, nu_ref, x_ref, wg_ref, wu_ref, wd_ref, o_ref, wgb_ref, wub_ref, wdb_ref):
    i = pl.program_id(0)
    cur = jnp.minimum(i, nu_ref[0] - 1)
    new_expert = jnp.logical_or(i == 0, be_ref[cur] != be_ref[jnp.maximum(cur - 1, 0)])

    @pl.when(new_expert)
    def _():
        wgb_ref[...] = wg_ref[0, 0].astype(BF16)
        wub_ref[...] = wu_ref[0, 0].astype(BF16)
        wdb_ref[...] = wd_ref[0, 0].astype(BF16)

    @pl.when(i < nu_ref[0])
    def _():
        xb = _tiles_to_rows(x_ref, MOE_BLK).astype(BF16)
        hg = _dot(xb, wgb_ref[...])
        hu = _dot(xb, wub_ref[...])
        hid = (hg * jax.nn.sigmoid(hg) * hu).astype(BF16)
        _rows_to_tiles(o_ref, _dot(hid, wdb_ref[...]))

    @pl.when(i >= nu_ref[0])
    def _():
        o_ref[...] = jnp.zeros_like(o_ref)


def _experts(xbuf, blk_exp, n_used, wg, wu, wd, layer):
    n_blk = xbuf.shape[0] // (MOE_BLK * ROW_TILE)
    blk = lambda i, be, nu: (jnp.minimum(i, nu[0] - 1), 0)
    oblk = lambda i, be, nu: (i, 0)
    wmap = lambda i, be, nu: (layer, be[jnp.minimum(i, nu[0] - 1)], 0, 0)
    return pl.pallas_call(
        _expert_kernel,
        grid_spec=pltpu.PrefetchScalarGridSpec(
            num_scalar_prefetch=2,
            grid=(n_blk,),
            in_specs=[pl.BlockSpec((MOE_BLK * ROW_TILE, LANES), blk),
                      pl.BlockSpec((1, 1, D_MODEL, D_EXPERT), wmap),
                      pl.BlockSpec((1, 1, D_MODEL, D_EXPERT), wmap),
                      pl.BlockSpec((1, 1, D_EXPERT, D_MODEL), wmap)],
            out_specs=pl.BlockSpec((MOE_BLK * ROW_TILE, LANES), oblk),
            scratch_shapes=[pltpu.VMEM((D_MODEL, D_EXPERT), BF16), pltpu.VMEM((D_MODEL, D_EXPERT), BF16),
                            pltpu.VMEM((D_EXPERT, D_MODEL), BF16)],
        ),
        out_shape=jax.ShapeDtypeStruct(xbuf.shape, F32),
        compiler_params=_cparams(("arbitrary",)),
        name="moe_experts",
    )(blk_exp, n_used, xbuf, wg, wu, wd)


def _combine_kernel(d0_ref, d1_ref, ybuf_ref, x_ref, mod_ref, route_ref, sel_ref, lng_ref, lnb_ref, *rest,
                    alpha, split_blk):
    o_refs, (ya_ref, yb_ref, sem) = rest[:-3], rest[-3:]

    def body(t, carry):
        row = pl.ds(pl.multiple_of(t * ROW_TILE, ROW_TILE), ROW_TILE)
        for k, (d_ref, y_ref) in enumerate(((d0_ref, ya_ref), (d1_ref, yb_ref))):
            src = ybuf_ref.at[pl.ds(pl.multiple_of(d_ref[0, t], ROW_TILE), ROW_TILE)]
            pltpu.make_async_copy(src, y_ref.at[row], sem.at[k]).start(priority=k)
        return carry

    lax.fori_loop(0, TM, body, 0, unroll=8)
    gates = _dot_tn(route_ref[...].astype(BF16), sel_ref[...])
    for k, y_ref in enumerate((ya_ref, yb_ref)):
        pltpu.make_async_copy(ybuf_ref.at[pl.ds(0, TM * ROW_TILE)], y_ref, sem.at[k]).wait()
    m = mod_ref[0]
    g0 = jnp.concatenate([gates[:, 0:LANES]] * ROW_TILE, axis=1)
    g1 = jnp.concatenate([gates[:, LANES:]] * ROW_TILE, axis=1)
    y = _tiles_to_rows(ya_ref, TM) * g0 + _tiles_to_rows(yb_ref, TM) * g1
    out = _layer_norm(alpha * x_ref[...] + (1.0 + m[5:6]) * y, lng_ref[...], lnb_ref[...])
    if split_blk is None:
        o_refs[0][...] = out
    else:
        @pl.when(pl.program_id(0) < split_blk)
        def _():
            o_refs[0][...] = out

        @pl.when(pl.program_id(0) >= split_blk)
        def _():
            o_refs[1][...] = out


def _gate_select():
    sel = np.zeros((ROUTE_ROWS, 2 * LANES), np.float32)
    sel[4:7, 0:LANES] = 1.0
    sel[7:10, LANES:] = 1.0
    return jnp.asarray(sel, BF16)


def _combine(ybuf, dest, route, x, mod_l, ln_g, ln_b, segs, *, alpha, split=False):
    nt, d = x.shape
    const = lambda i: (0, 0)
    smem_row = pl.BlockSpec((1, TM), lambda i: (0, i), memory_space=pltpu.SMEM)
    if split:
        n0 = segs[0][0] * segs[0][1]
        split_blk = n0 // TM
        out_specs = [pl.BlockSpec((TM, d), lambda i: (jnp.minimum(i, split_blk - 1), 0)),
                     pl.BlockSpec((TM, d), lambda i: (jnp.maximum(i - split_blk, 0), 0))]
        out_shape = [jax.ShapeDtypeStruct((n0, d), F32), jax.ShapeDtypeStruct((nt - n0, d), F32)]
    else:
        split_blk = None
        out_specs = [pl.BlockSpec((TM, d), lambda i: (i, 0))]
        out_shape = [jax.ShapeDtypeStruct((nt, d), F32)]
    res = pl.pallas_call(
        functools.partial(_combine_kernel, alpha=alpha, split_blk=split_blk),
        grid=(nt // TM,),
        in_specs=[smem_row, smem_row,
                  pl.BlockSpec(memory_space=pl.ANY),
                  pl.BlockSpec((TM, d), lambda i: (i, 0)),
                  pl.BlockSpec((1, 6, d), lambda i: (_mod_row(i, TM, segs), 0, 0)),
                  pl.BlockSpec((ROUTE_ROWS, TM), lambda i: (0, i)),
                  pl.BlockSpec((ROUTE_ROWS, 2 * LANES), const),
                  pl.BlockSpec((1, d), const), pl.BlockSpec((1, d), const)],
        out_specs=out_specs,
        out_shape=out_shape,
        scratch_shapes=[pltpu.VMEM((TM * ROW_TILE, LANES), F32), pltpu.VMEM((TM * ROW_TILE, LANES), F32),
                        pltpu.SemaphoreType.DMA((2,))],
        compiler_params=_cparams(("arbitrary",)),
        name="moe_combine",
    )(dest[0], dest[1], ybuf, x, mod_l, route, _gate_select(), ln_g.reshape(1, d), ln_b.reshape(1, d))
    return res if split else res[0]


def _moe_slots(nt):
    return 2 * nt + N_EXPERTS * MOE_BLK


def _moe(x, mod_l, route, counts, ln_g, ln_b, wg, wu, wd, layer, xbuf, segs, *, alpha, split=False):
    n_blk = _moe_slots(x.shape[0]) // MOE_BLK
    cnt = counts[:, 0].astype(jnp.int32)
    padded = (cnt + MOE_BLK - 1) // MOE_BLK * MOE_BLK
    pad_ends = jnp.cumsum(padded)
    pad_starts = (pad_ends - padded).astype(jnp.int32)
    blk_start = jnp.arange(n_blk, dtype=jnp.int32) * MOE_BLK
    blk_exp = jnp.minimum(jnp.sum(pad_ends[None, :] <= blk_start[:, None], axis=1), N_EXPERTS - 1).astype(jnp.int32)
    n_used = (pad_ends[-1:] // MOE_BLK).astype(jnp.int32)
    dest = _dest(route, pad_starts)
    xbuf = _dispatch(x, mod_l, dest, xbuf, segs)
    ybuf = _experts(xbuf, blk_exp, n_used, wg, wu, wd, layer)
    return _combine(ybuf, dest, route, x, mod_l, ln_g, ln_b, segs, alpha=alpha, split=split), xbuf


def _attn_w_in_layout(w_in):
    scale = HEAD_DIM ** -0.5
    qa = w_in[:, 0:512] * scale
    ka = w_in[:, 512:640]
    va = w_in[:, 640:768]
    qb = w_in[:, 768:1280] * scale
    kb = w_in[:, 1280:1792]
    vb = w_in[:, 1792:2304]
    dup = lambda a: jnp.concatenate([a[:, 0:64], a[:, 0:64], a[:, 64:128], a[:, 64:128]], axis=1)
    return jnp.concatenate([qa, dup(ka), dup(va), qb, kb, vb], axis=1).astype(BF16)


def _rec_w_in_layout(w_in):
    d = D_MODEL
    return jnp.concatenate([w_in[:, d:3 * d], w_in[:, 0:d], w_in[:, 3 * d:5 * d]], axis=1).astype(BF16)


def _rope_tables(t_max):
    inv = jnp.power(ROPE_THETA, -jnp.arange(0, HEAD_DIM, 2, dtype=F32) / HEAD_DIM)
    ang = jnp.arange(t_max, dtype=F32)[:, None] * inv[None, :]
    cos = jnp.cos(ang)
    sin = jnp.sin(ang)
    cos128 = jnp.concatenate([cos, cos, cos, cos], axis=1)
    sin128 = jnp.concatenate([-sin, sin, -sin, sin], axis=1)
    return cos128, sin128


def kernel(x_prompt, x_sample, c_prompt, c_sample, ada_w, ada_b, ln_g, ln_b, attn_w_in, attn_sink, nat_rel_bias, attn_w_out, rec_w_in, rec_lb, rec_gnorm, rec_w_out, router_w_group, router_b_group, router_w_expert, router_b_expert, expert_w_gate, expert_w_up, expert_w_down):
    depth = ada_w.shape[0]
    d = x_prompt.shape[-1]
    segs = ((x_prompt.shape[0], x_prompt.shape[1]), (x_sample.shape[0], x_sample.shape[1]))
    alpha = (2 * depth) ** 0.25
    x = jnp.concatenate([x_prompt.reshape(-1, d), x_sample.reshape(-1, d)], axis=0)
    c = jnp.concatenate([c_prompt, c_sample], axis=0)
    nseq = c.shape[0]
    mod = _adaln(c, ada_w, ada_b).reshape(depth, nseq, 6, d)
    rope_tables = _rope_tables(max(t for _, t in segs))
    xbuf = jnp.zeros((_moe_slots(x.shape[0]) * ROW_TILE, LANES), F32)
    for l in range(depth):
        mod_l = mod[l]
        i = l // 2
        router = _router_tables(router_w_group[l], router_b_group[l], router_w_expert[l], router_b_expert[l])
        if l % 2 == 0:
            proj = _inproj_attn(x, mod_l, _attn_w_in_layout(attn_w_in[i]), segs, rope_tables)
            oa = _win_attn(proj, attn_sink[i].astype(F32), segs)
            ob = _nat_attn(proj, _nat_bias_table(nat_rel_bias[i]), segs)
            x, route, counts = _mix_out((oa, ob), attn_w_out[i].astype(BF16), x, mod_l, ln_g[l, 0], ln_b[l, 0],
                                        router, segs, alpha=alpha)
        else:
            proj_z, proj_qvg = _inproj_rec(x, mod_l, _rec_w_in_layout(rec_w_in[i]), segs)
            o_f, o_b = _hgrn(proj_z, proj_qvg, rec_lb.astype(F32), i, segs)
            x, route, counts = _mix_out((o_f, o_b), rec_w_out[i].astype(BF16), x, mod_l, ln_g[l, 0], ln_b[l, 0],
                                        router, segs, alpha=alpha, rec_extra=(proj_qvg, rec_gnorm[i].astype(F32)))
        x, xbuf = _moe(x, mod_l, route, counts, ln_g[l, 1], ln_b[l, 1], expert_w_gate, expert_w_up, expert_w_down,
                       l, xbuf, segs, alpha=alpha, split=(l == depth - 1))
    return (x[0].reshape(x_prompt.shape), x[1].reshape(x_sample.shape))
```

```python
import functools

import numpy as np
import jax
import jax.numpy as jnp
from jax import lax
from jax.experimental import pallas as pl
from jax.experimental.pallas import tpu as pltpu

F32 = jnp.float32
BF16 = jnp.bfloat16
NEG_INF = float("-inf")

D_MODEL = 1024
HEAD_DIM = 64
A_HEADS = 8
A_KV_HEADS = 2
A_WINDOW = 128
A_BLOCK = 128
B_HEADS = 8
GRID_W = 64
NB_ROWS = 8
NB_COLS = 16
ROPE_THETA = 10000.0
C_HEADS = 8
C_KEY_DIM = 128
C_CHUNK = 64
N_GROUPS = 4
EXPERTS_PER_GROUP = 8
N_EXPERTS = 32
D_EXPERT = 512
LN_EPS = 1e-5
RMS_EPS = 1e-6
LOG2E = 1.4426950408889634

LANES = 128
ATTN_W = 2560
ROPE_W = 768
REC_W = 5 * D_MODEL
ROUTER_ROWS = 40
ROUTE_ROWS = 16
ROW_TILE = D_MODEL // LANES

TM = 512
TM_IN = 1024
MOE_BLK = 512
NAT_BLK = 512
REC_BLK = 512
VMEM_LIMIT = 48 * 1024 * 1024


def _cparams(sem):
    return pltpu.CompilerParams(dimension_semantics=sem, vmem_limit_bytes=VMEM_LIMIT)


def _decode(i, blk, segs):
    out = None
    blk0 = 0
    for (b, t) in segs:
        nb = t // blk
        j = (i - blk0) % nb
        cand = (i - j, j, nb)
        out = cand if out is None else tuple(jnp.where(i < blk0, o, c) for o, c in zip(out, cand))
        blk0 += b * nb
    return out


def _mod_row(i, blk, segs):
    out = None
    blk0 = 0
    row0 = 0
    for (b, t) in segs:
        nb = t // blk
        cand = row0 + (i - blk0) // nb
        out = cand if out is None else jnp.where(i < blk0, out, cand)
        blk0 += b * nb
        row0 += b
    return out


def _pos_block(i, blk, segs):
    return _decode(i, blk, segs)[1]


def _dot(a, b):
    return jnp.dot(a, b, preferred_element_type=F32)


def _dot_nt(a, b):
    return lax.dot_general(a, b, (((1,), (1,)), ((), ())), preferred_element_type=F32)


def _dot_tn(a, b):
    return lax.dot_general(a, b, (((0,), (0,)), ((), ())), preferred_element_type=F32)


def _split2(x):
    hi = x.astype(BF16)
    lo = (x - hi.astype(F32)).astype(BF16)
    return hi, lo


def _layer_norm(x, g, b):
    mu = jnp.mean(x, axis=-1, keepdims=True)
    xc = x - mu
    var = jnp.mean(xc * xc, axis=-1, keepdims=True)
    return xc * lax.rsqrt(var + LN_EPS) * g + b


def _adaln_kernel(c_ref, w_ref, b_ref, o_ref):
    c = c_ref[...]
    cs = c * jax.nn.sigmoid(c)
    c_hi, c_lo = _split2(cs)
    w_hi, w_lo = _split2(w_ref[0])
    o_ref[0] = _dot(c_hi, w_hi) + _dot(c_hi, w_lo) + _dot(c_lo, w_hi) + b_ref[0]


def _adaln(c, ada_w, ada_b):
    depth, d, n = ada_w.shape
    nb = c.shape[0]
    tn = 1536
    return pl.pallas_call(
        _adaln_kernel,
        grid=(depth, n // tn),
        in_specs=[
            pl.BlockSpec((nb, d), lambda l, j: (0, 0)),
            pl.BlockSpec((1, d, tn), lambda l, j: (l, 0, j)),
            pl.BlockSpec((1, 1, tn), lambda l, j: (l, 0, j)),
        ],
        out_specs=pl.BlockSpec((1, nb, tn), lambda l, j: (l, 0, j)),
        out_shape=jax.ShapeDtypeStruct((depth, nb, n), F32),
        compiler_params=_cparams(("parallel", "parallel")),
        name="adaln",
    )(c, ada_w, ada_b.reshape(depth, 1, n))


def _rot_half_pairs(x):
    lane = lax.broadcasted_iota(jnp.int32, x.shape, 1)
    first = (lane & 63) < 32
    return jnp.where(first, pltpu.roll(x, 96, 1), pltpu.roll(x, 32, 1))


def _modulate_once(x_ref, mod_ref, xb_ref):
    @pl.when(pl.program_id(1) == 0)
    def _():
        m = mod_ref[0]
        xb_ref[...] = (x_ref[...] * (1.0 + m[1:2]) + m[0:1]).astype(BF16)


def _inproj_attn_kernel(x_ref, mod_ref, w_ref, cos_ref, sin_ref, o_ref, xb_ref):
    _modulate_once(x_ref, mod_ref, xb_ref)
    y = _dot(xb_ref[...], w_ref[...])

    @pl.when(pl.program_id(1) == 0)
    def _():
        cos = cos_ref[...]
        sin = sin_ref[...]
        for g in range(ROPE_W // LANES):
            blk = y[:, g * LANES:(g + 1) * LANES]
            o_ref[:, g * LANES:(g + 1) * LANES] = (blk * cos + _rot_half_pairs(blk) * sin).astype(o_ref.dtype)
        o_ref[:, ROPE_W:] = y[:, ROPE_W:].astype(o_ref.dtype)

    @pl.when(pl.program_id(1) != 0)
    def _():
        o_ref[...] = y.astype(o_ref.dtype)


def _inproj_rec_kernel(x_ref, mod_ref, w_ref, oz_ref, oq_ref, xb_ref, *, nz):
    _modulate_once(x_ref, mod_ref, xb_ref)
    y = _dot(xb_ref[...], w_ref[...])

    @pl.when(pl.program_id(1) < nz)
    def _():
        oz_ref[...] = y

    @pl.when(pl.program_id(1) >= nz)
    def _():
        oq_ref[...] = y.astype(oq_ref.dtype)


def _inproj_specs(x, segs, tn):
    d = x.shape[1]
    return [pl.BlockSpec((TM_IN, d), lambda i, j: (i, 0)),
            pl.BlockSpec((1, 6, d), lambda i, j: (_mod_row(i, TM_IN, segs), 0, 0)),
            pl.BlockSpec((d, tn), lambda i, j: (0, j))]


def _inproj_attn(x, mod_l, w, segs, rope_tables):
    nt, d = x.shape
    n = w.shape[1]
    tn = n // 2
    tab = pl.BlockSpec((TM_IN, LANES), lambda i, j: (_pos_block(i, TM_IN, segs), 0))
    return pl.pallas_call(
        _inproj_attn_kernel,
        grid=(nt // TM_IN, n // tn),
        in_specs=_inproj_specs(x, segs, tn) + [tab, tab],
        out_specs=pl.BlockSpec((TM_IN, tn), lambda i, j: (i, j)),
        out_shape=jax.ShapeDtypeStruct((nt, n), BF16),
        scratch_shapes=[pltpu.VMEM((TM_IN, d), BF16)],
        compiler_params=_cparams(("parallel", "arbitrary")),
        name="inproj_rope",
    )(x, mod_l, w, *rope_tables)


def _inproj_rec(x, mod_l, w, segs):
    nt, d = x.shape
    tn = D_MODEL
    nz = 2
    nq = w.shape[1] // tn - nz
    return pl.pallas_call(
        functools.partial(_inproj_rec_kernel, nz=nz),
        grid=(nt // TM_IN, nz + nq),
        in_specs=_inproj_specs(x, segs, tn),
        out_specs=[pl.BlockSpec((TM_IN, tn), lambda i, j: (i, jnp.minimum(j, nz - 1))),
                   pl.BlockSpec((TM_IN, tn), lambda i, j: (i, jnp.maximum(j - nz, 0)))],
        out_shape=[jax.ShapeDtypeStruct((nt, nz * tn), F32), jax.ShapeDtypeStruct((nt, nq * tn), BF16)],
        scratch_shapes=[pltpu.VMEM((TM_IN, d), BF16)],
        compiler_params=_cparams(("parallel", "arbitrary")),
        name="inproj_rec",
    )(x, mod_l, w)


def _win_attn_kernel(sink_ref, q_ref, k0_ref, k1_ref, k2_ref, v0_ref, v1_ref, v2_ref, o_ref, *, segs):
    i = pl.program_id(0)
    _, j, nb = _decode(i, A_BLOCK, segs)
    ws = jnp.clip(j - 1, 0, nb - 3)
    span = 3 * A_BLOCK
    qpos = j * A_BLOCK + lax.broadcasted_iota(jnp.int32, (A_BLOCK, span), 0)
    kpos = ws * A_BLOCK + lax.broadcasted_iota(jnp.int32, (A_BLOCK, span), 1)
    valid = jnp.abs(qpos - kpos) <= A_WINDOW
    lo = lax.broadcasted_iota(jnp.int32, (A_BLOCK, LANES), 1) < HEAD_DIM
    kfull = jnp.concatenate([k0_ref[...], k1_ref[...], k2_ref[...]], axis=0)
    vfull = jnp.concatenate([v0_ref[...], v1_ref[...], v2_ref[...]], axis=0)
    zero = jnp.zeros((A_BLOCK, LANES), BF16)
    grp = A_HEADS // A_KV_HEADS
    for kvh in range(A_KV_HEADS):
        kk = kfull[:, kvh * LANES:(kvh + 1) * LANES]
        vv = vfull[:, kvh * LANES:(kvh + 1) * LANES]
        parts = []
        for p in range(grp // 2):
            c0 = (kvh * (grp // 2) + p) * LANES
            qp = q_ref[:, c0:c0 + LANES]
            parts.append(jnp.where(lo, qp, zero))
            parts.append(jnp.where(lo, zero, qp))
        qs = jnp.concatenate(parts, axis=0)
        s = _dot_nt(qs, kk)
        probs = []
        inv = []
        for h in range(grp):
            sh = jnp.where(valid, s[h * A_BLOCK:(h + 1) * A_BLOCK], NEG_INF)
            sink = sink_ref[kvh * grp + h]
            m = jnp.maximum(jnp.max(sh, axis=-1, keepdims=True), sink)
            e = jnp.exp(sh - m)
            den = jnp.sum(e, axis=-1, keepdims=True) + jnp.exp(sink - m)
            probs.append(e.astype(BF16))
            inv.append(1.0 / den)
        o = _dot(jnp.concatenate(probs, axis=0), vv)
        for p in range(grp // 2):
            oe = o[(2 * p) * A_BLOCK:(2 * p + 1) * A_BLOCK] * inv[2 * p]
            oo = o[(2 * p + 1) * A_BLOCK:(2 * p + 2) * A_BLOCK] * inv[2 * p + 1]
            c0 = (kvh * (grp // 2) + p) * LANES
            o_ref[:, c0:c0 + LANES] = jnp.where(lo, oe, oo).astype(o_ref.dtype)


def _win_attn(proj, sink, segs):
    nt = proj.shape[0]
    kvw = 2 * LANES

    def kv_map(d, col):
        def f(i):
            base, j, nb = _decode(i, A_BLOCK, segs)
            return (base + jnp.clip(j - 1, 0, nb - 3) + d, col)
        return f

    in_specs = [pl.BlockSpec(memory_space=pltpu.SMEM),
                pl.BlockSpec((A_BLOCK, 512), lambda i: (i, 0))]
    in_specs += [pl.BlockSpec((A_BLOCK, kvw), kv_map(d, 2)) for d in range(3)]
    in_specs += [pl.BlockSpec((A_BLOCK, kvw), kv_map(d, 3)) for d in range(3)]
    return pl.pallas_call(
        functools.partial(_win_attn_kernel, segs=segs),
        grid=(nt // A_BLOCK,),
        in_specs=in_specs,
        out_specs=pl.BlockSpec((A_BLOCK, 512), lambda i: (i, 0)),
        out_shape=jax.ShapeDtypeStruct((nt, 512), BF16),
        compiler_params=_cparams(("parallel",)),
        name="win_attn",
    )(sink, proj, proj, proj, proj, proj, proj, proj)


def _nat_bias_table(rel_bias):
    c = np.arange(GRID_W)
    kc = np.arange(GRID_W)
    qwin = np.clip(c - NB_COLS // 2, 0, GRID_W - NB_COLS)
    ok = (kc[None, :] >= qwin[:, None]) & (kc[None, :] < qwin[:, None] + NB_COLS)
    dc = np.clip(kc[None, :] - c[:, None], -(NB_COLS - 1), NB_COLS - 1) + NB_COLS - 1
    var = np.arange(NB_ROWS)
    irow = np.arange(NB_ROWS)
    dr = irow[None, :] - var[:, None] + NB_ROWS - 1
    t = rel_bias.astype(F32)[:, dr][:, :, :, dc]
    t = jnp.where(jnp.asarray(ok)[None, None, None], t, NEG_INF)
    t = t.transpose(1, 0, 3, 2, 4)
    return t.reshape(NB_ROWS, B_HEADS, GRID_W, NB_ROWS * GRID_W)


def _nat_attn_kernel(q_ref, k0, k1, k2, k3, v0, v1, v2, v3, bias_ref, o_ref, kbuf, vbuf, *, segs):
    i = pl.program_id(0)
    _, j, nb = _decode(i, NAT_BLK, segs)
    rows = nb * NB_ROWS
    w0 = jnp.clip(NB_ROWS * j - NB_ROWS // 2, 0, rows - 2 * NB_ROWS)
    for d, (kr, vr) in enumerate(((k0, v0), (k1, v1), (k2, v2), (k3, v3))):
        kbuf[d * 256:(d + 1) * 256, :] = kr[...]
        vbuf[d * 256:(d + 1) * 256, :] = vr[...]
    lo = lax.broadcasted_iota(jnp.int32, (GRID_W, LANES), 1) < HEAD_DIM
    zero = jnp.zeros((GRID_W, LANES), BF16)
    nkeys = NB_ROWS * GRID_W

    def scores(rr, hp):
        r = NB_ROWS * j + rr
        rs = jnp.clip(r - NB_ROWS // 2, 0, rows - NB_ROWS)
        off = pl.multiple_of((rs - w0) * GRID_W, GRID_W)
        cs = slice(hp * LANES, (hp + 1) * LANES)
        qp = q_ref[rr * GRID_W:(rr + 1) * GRID_W, cs]
        qs = jnp.concatenate([jnp.where(lo, qp, zero), jnp.where(lo, zero, qp)], axis=0)
        s = _dot_nt(qs, kbuf[pl.ds(off, nkeys), cs])
        return s, r - rs, off

    units = [(rr, hp) for rr in range(NB_ROWS) for hp in range(B_HEADS // 2)]
    nxt = scores(*units[0])
    for u, (rr, hp) in enumerate(units):
        s, var, off = nxt
        if u + 1 < len(units):
            nxt = scores(*units[u + 1])
        cs = slice(hp * LANES, (hp + 1) * LANES)
        s = s + jnp.concatenate([bias_ref[var, 2 * hp], bias_ref[var, 2 * hp + 1]], axis=0)
        m = jnp.max(s, axis=-1, keepdims=True)
        e = jnp.exp(s - m)
        inv = 1.0 / jnp.sum(e, axis=-1, keepdims=True)
        o = _dot(e.astype(BF16), vbuf[pl.ds(off, nkeys), cs]) * inv
        o_ref[rr * GRID_W:(rr + 1) * GRID_W, cs] = jnp.where(lo, o[:GRID_W], o[GRID_W:]).astype(o_ref.dtype)


def _nat_attn(proj, bias_table, segs):
    nt = proj.shape[0]

    def kv_map(d, col):
        def f(i):
            base, j, nb = _decode(i, NAT_BLK, segs)
            return (2 * base + jnp.clip(2 * j - 1, 0, 2 * nb - 4) + d, col)
        return f

    in_specs = [pl.BlockSpec((NAT_BLK, 512), lambda i: (i, 2))]
    in_specs += [pl.BlockSpec((256, 512), kv_map(d, 3)) for d in range(4)]
    in_specs += [pl.BlockSpec((256, 512), kv_map(d, 4)) for d in range(4)]
    in_specs += [pl.BlockSpec(bias_table.shape, lambda i: (0, 0, 0, 0))]
    return pl.pallas_call(
        functools.partial(_nat_attn_kernel, segs=segs),
        grid=(nt // NAT_BLK,),
        in_specs=in_specs,
        out_specs=pl.BlockSpec((NAT_BLK, 512), lambda i: (i, 0)),
        out_shape=jax.ShapeDtypeStruct((nt, 512), BF16),
        scratch_shapes=[pltpu.VMEM((1024, 512), BF16), pltpu.VMEM((1024, 512), BF16)],
        compiler_params=_cparams(("parallel",)),
        name="nat_attn",
    )(*([proj] * 9), bias_table)


def _chunk_cumsum(x, fwd):
    n = x.shape[0]
    x3 = x.reshape(n // 8, 8, LANES)
    sub = lax.broadcasted_iota(jnp.int32, x3.shape, 1)
    s = 1
    while s < 8:
        if fwd:
            x3 = x3 + jnp.where(sub >= s, pltpu.roll(x3, s, 1), 0.0)
        else:
            x3 = x3 + jnp.where(sub < 8 - s, pltpu.roll(x3, 8 - s, 1), 0.0)
        s *= 2
    nv = n // 8
    outs = [None] * nv
    edges = [None] * nv
    run = None
    edge = 7 if fwd else 0
    for j in (range(nv) if fwd else range(nv - 1, -1, -1)):
        cur = x3[j]
        if run is not None:
            cur = cur + run
        outs[j] = cur
        run = jnp.broadcast_to(cur[edge:edge + 1, :], (8, LANES))
        edges[j] = run
    return outs, edges


def _split_rows(outs, edges, m, fwd):
    nv = len(outs)
    if m >= 8:
        g = m // 8
        res = []
        for j in range(nv):
            first = (j // (2 * g)) * 2 * g
            res.append(edges[first + g - 1] if fwd else edges[first + g])
        return res
    sub = lax.broadcasted_iota(jnp.int32, (8, LANES), 0)
    pick = m - 1 if fwd else m
    res = []
    for j in range(nv):
        r = None
        for gi in reversed(range(8 // (2 * m))):
            row = gi * 2 * m + pick
            bc = jnp.broadcast_to(outs[j][row:row + 1, :], (8, LANES))
            r = bc if r is None else jnp.where(sub < (gi + 1) * 2 * m, bc, r)
        res.append(r)
    return res


def _hgrn_chunk(q, z, vb, lower, masks, fwd):
    n = C_CHUNK
    f = lower + (1.0 - lower) * jax.nn.sigmoid(z)
    kk = 1.0 - f
    outs, edges = _chunk_cumsum(jnp.log2(f), fwd)
    b = jnp.concatenate(outs, axis=0)
    total = b[n - 1:n] if fwd else b[0:1]
    rowi = lax.broadcasted_iota(jnp.int32, (n, 1), 0)
    zero = jnp.zeros((n, LANES), BF16)
    xs = []
    m = n // 2
    while m >= 2:
        r = jnp.concatenate(_split_rows(outs, edges, m, fwd), axis=0)
        upper = (rowi & (2 * m - 1)) >= m
        qside = upper if fwd else jnp.logical_not(upper)
        e = jnp.exp2((b - r) * masks[N_PAIR_MASKS + len(xs)])
        xs.append((jnp.where(qside, q, kk) * e).astype(BF16))
        m //= 2
    odd = (rowi & 1) == 1
    qside = odd if fwd else jnp.logical_not(odd)
    xs.append(jnp.where(qside, q * f, kk).astype(BF16))
    ti = lax.broadcasted_iota(jnp.int32, (n, LANES), 0)
    li = lax.broadcasted_iota(jnp.int32, (n, LANES), 1)
    asum = jnp.where(ti == li, jnp.sum(q * kk, axis=-1, keepdims=True), 0.0)
    for p in range(len(xs) // 2):
        x1, x2 = xs[2 * p], xs[2 * p + 1]
        lhs = jnp.concatenate([x1, x2], axis=1)
        rhs = jnp.concatenate([jnp.concatenate([x1, zero], axis=1), jnp.concatenate([zero, x2], axis=1)], axis=0)
        asum = asum + _dot_nt(lhs, rhs) * masks[p]
    ke = (kk * jnp.exp2(total - b)).astype(BF16)
    u = _dot_tn(vb, ke)
    o = _dot(asum.astype(BF16), jnp.concatenate([vb, vb], axis=0))
    qe = (q * jnp.exp2(b)).astype(BF16)
    return o, qe, u, jnp.exp2(total)


N_PAIR_MASKS = 3


def _level_masks(fwd):
    n = C_CHUNK
    t = np.arange(n)[:, None]
    s = np.arange(n)[None, :]
    out = np.zeros((N_PAIR_MASKS + 5, n, 2 * n), np.float32)
    for i, m in enumerate([32, 16, 8, 4, 2, 1]):
        same = (t // (2 * m)) == (s // (2 * m))
        tu = (t % (2 * m)) >= m
        su = (s % (2 * m)) >= m
        ok = same & ((tu & ~su) if fwd else (~tu & su))
        out[i // 2, :, (i % 2) * n:(i % 2 + 1) * n] = ok
        if m >= 2:
            out[N_PAIR_MASKS + i] = np.where(tu if fwd else ~tu, 1.0, -1.0)
    return out


def _hgrn_kernel(lb_ref, mf_ref, mb_ref, qf_ref, zf_ref, vf_ref, qb_ref, zb_ref, vb_ref,
                 of_ref, ob_ref, sf_ref, sb_ref, *, segs, layer):
    i = pl.program_id(1)
    _, j, _ = _decode(i, REC_BLK, segs)

    @pl.when(j == 0)
    def _():
        sf_ref[...] = jnp.zeros_like(sf_ref)
        sb_ref[...] = jnp.zeros_like(sb_ref)

    lb = lb_ref[...]
    ex = jnp.exp(lb - jnp.max(lb, axis=0, keepdims=True))
    p = ex / jnp.sum(ex, axis=0, keepdims=True)
    lower = jnp.sum(p[0:layer + 1], axis=0, keepdims=True) - p[0:1]
    mf, mb = mf_ref, mb_ref
    nch = REC_BLK // C_CHUNK
    dirs = ((qf_ref, zf_ref, vf_ref, of_ref, sf_ref, mf, True), (qb_ref, zb_ref, vb_ref, ob_ref, sb_ref, mb, False))
    parts = ([], [])
    for c in range(nch):
        for d, (q_ref, z_ref, v_ref, _, _, masks, fwd) in enumerate(dirs):
            cc = c if fwd else nch - 1 - c
            sl = slice(cc * C_CHUNK, (cc + 1) * C_CHUNK)
            parts[d].append((sl,) + _hgrn_chunk(q_ref[sl, :].astype(F32), z_ref[sl, :], v_ref[sl, :], lower, masks, fwd))
    for d, (_, _, _, o_ref, s_ref, _, _) in enumerate(dirs):
        st = s_ref[...]
        for (sl, o, qe, u, dec) in parts[d]:
            o_ref[sl, :] = o + _dot_nt(qe, st.astype(BF16))
            st = st * dec + u
        s_ref[...] = st


def _hgrn(proj_z, proj_qvg, rec_lb, layer, segs):
    nt = proj_z.shape[0]
    nrec = rec_lb.shape[0]
    mf = jnp.asarray(_level_masks(True))
    mb = jnp.asarray(_level_masks(False))

    def fwd_map(col):
        return lambda h, i: (i, col * C_HEADS + h)

    def bwd_map(col):
        def f(h, i):
            base, j, nb = _decode(i, REC_BLK, segs)
            return (base + nb - 1 - j, col * C_HEADS + h)
        return f

    blk = (REC_BLK, C_KEY_DIM)
    c3 = lambda h, i: (0, 0, 0)
    in_specs = [pl.BlockSpec((nrec, C_KEY_DIM), lambda h, i: (0, h)),
                pl.BlockSpec(mf.shape, c3), pl.BlockSpec(mb.shape, c3),
                pl.BlockSpec(blk, fwd_map(0)), pl.BlockSpec(blk, fwd_map(0)), pl.BlockSpec(blk, fwd_map(1)),
                pl.BlockSpec(blk, bwd_map(0)), pl.BlockSpec(blk, bwd_map(1)), pl.BlockSpec(blk, bwd_map(1))]
    out_shape = [jax.ShapeDtypeStruct((nt, D_MODEL), F32)] * 2
    return pl.pallas_call(
        functools.partial(_hgrn_kernel, segs=segs, layer=layer),
        grid=(C_HEADS, nt // REC_BLK),
        in_specs=in_specs,
        out_specs=[pl.BlockSpec(blk, fwd_map(0)), pl.BlockSpec(blk, bwd_map(0))],
        out_shape=out_shape,
        scratch_shapes=[pltpu.VMEM((C_KEY_DIM, C_KEY_DIM), F32)] * 2,
        compiler_params=_cparams(("parallel", "arbitrary")),
        name="hgrn",
    )(rec_lb, mf, mb, proj_qvg, proj_z, proj_qvg, proj_qvg, proj_z, proj_qvg)


def _route_rows(logits_t, utri, cnt_ref):
    tm = logits_t.shape[1]
    gi = lax.broadcasted_iota(jnp.int32, (8, tm), 0).astype(F32)
    lg = jnp.where(gi < N_GROUPS, logits_t[0:8], NEG_INF)
    gm = jnp.max(lg, axis=0, keepdims=True)
    p_grp = 1.0 / jnp.sum(jnp.exp(lg - gm), axis=0, keepdims=True)
    grp = jnp.min(jnp.where(lg == gm, gi, 99.0), axis=0, keepdims=True)
    ei = lax.broadcasted_iota(jnp.int32, (N_EXPERTS, tm), 0).astype(F32)
    egrp = (lax.broadcasted_iota(jnp.int32, (N_EXPERTS, tm), 0) >> 3).astype(F32)
    le = jnp.where(egrp == grp, logits_t[8:8 + N_EXPERTS], NEG_INF)
    v1 = jnp.max(le, axis=0, keepdims=True)
    i1 = jnp.min(jnp.where(le == v1, ei, 99.0), axis=0, keepdims=True)
    le2 = jnp.where(ei == i1, NEG_INF, le)
    v2 = jnp.max(le2, axis=0, keepdims=True)
    i2 = jnp.min(jnp.where(le2 == v2, ei, 99.0), axis=0, keepdims=True)
    t = jnp.exp(v2 - v1)
    g1 = p_grp / (1.0 + t)
    g2 = p_grp * t / (1.0 + t)
    o1 = jnp.where(ei == i1, 1.0, 0.0)
    o2 = jnp.where(ei == i2, 1.0, 0.0)
    osum = o1 + o2
    before = _dot(osum.astype(BF16), utri) + cnt_ref[:, 0:1]
    rank1 = jnp.sum(o1 * before, axis=0, keepdims=True)
    rank2 = jnp.sum(o2 * before, axis=0, keepdims=True)
    cnt_ref[...] = cnt_ref[...] + jnp.sum(osum, axis=1, keepdims=True)

    def pieces(g):
        hi = g.astype(BF16).astype(F32)
        r = g - hi
        mid = r.astype(BF16).astype(F32)
        return [hi, mid, r - mid]

    rows = [i1, i2, rank1, rank2] + pieces(g1) + pieces(g2)
    return rows + [jnp.zeros((1, tm), F32)] * (ROUTE_ROWS - len(rows))


def _mix_out_kernel(*refs, rec, alpha):
    if rec:
        (a_ref, b_ref, g_ref, gn_ref, w_ref, x_ref, mod_ref, lng_ref, lnb_ref,
         rw_hi_ref, rw_lo_ref, rb_ref, utri_ref, xo_ref, route_ref, cnt_out_ref, cnt_ref) = refs
    else:
        (a_ref, b_ref, w_ref, x_ref, mod_ref, lng_ref, lnb_ref,
         rw_hi_ref, rw_lo_ref, rb_ref, utri_ref, xo_ref, route_ref, cnt_out_ref, cnt_ref) = refs
    i = pl.program_id(0)

    @pl.when(i == 0)
    def _():
        cnt_ref[...] = jnp.zeros_like(cnt_ref)

    if rec:
        o = a_ref[...] + b_ref[...]
        g = g_ref[...].astype(F32)
        gate = gn_ref[...] * (g * jax.nn.sigmoid(g))
        parts = []
        for h in range(C_HEADS):
            blk = o[:, h * C_KEY_DIM:(h + 1) * C_KEY_DIM]
            ms = jnp.mean(blk * blk, axis=-1, keepdims=True)
            parts.append(blk * lax.rsqrt(ms + RMS_EPS))
        lhs = (jnp.concatenate(parts, axis=1) * gate).astype(BF16)
        y = _dot(lhs, w_ref[...])
    else:
        half = a_ref.shape[1]
        y = _dot(a_ref[...], w_ref[0:half, :]) + _dot(b_ref[...], w_ref[half:, :])
    m = mod_ref[0]
    xn = _layer_norm(alpha * x_ref[...] + (1.0 + m[2:3]) * y, lng_ref[...], lnb_ref[...])
    xo_ref[...] = xn
    h_hi, h_lo = _split2(xn * (1.0 + m[4:5]) + m[3:4])
    rw_hi = rw_hi_ref[...]
    logits_t = _dot_nt(rw_hi, h_hi) + _dot_nt(rw_hi, h_lo) + _dot_nt(rw_lo_ref[...], h_hi) + rb_ref[:, 0:1]
    for k, row in enumerate(_route_rows(logits_t, utri_ref[...], cnt_ref)):
        route_ref[k:k + 1, :] = row
    cnt_out_ref[...] = cnt_ref[...]


def _router_tables(w_rg, b_rg, w_re, b_re):
    d = w_rg.shape[0]
    wt = jnp.zeros((ROUTER_ROWS, d), F32)
    wt = wt.at[0:N_GROUPS].set(w_rg.astype(F32).T).at[8:8 + N_EXPERTS].set(w_re.astype(F32).T)
    hi = wt.astype(BF16)
    lo = (wt - hi.astype(F32)).astype(BF16)
    rb = jnp.zeros((ROUTER_ROWS,), F32).at[0:N_GROUPS].set(b_rg.astype(F32))
    rb = rb.at[8:8 + N_EXPERTS].set(b_re.astype(F32).reshape(-1))
    return hi, lo, jnp.broadcast_to(rb[:, None], (ROUTER_ROWS, LANES))


def _mix_out(lhs, w_out, x, mod_l, ln_g, ln_b, router, segs, *, alpha, rec_extra=None):
    nt, d = x.shape
    rec = rec_extra is not None
    rw_hi, rw_lo, rb = router
    idx = np.arange(TM)
    utri = jnp.asarray(idx[:, None] < idx[None, :], BF16)
    row = lambda i: (i, 0)
    const = lambda i: (0, 0)
    a, b = lhs
    in_specs = [pl.BlockSpec((TM, a.shape[1]), row), pl.BlockSpec((TM, b.shape[1]), row)]
    args = [a, b]
    if rec:
        proj, gnorm = rec_extra
        in_specs += [pl.BlockSpec((TM, d), lambda i: (i, 2)), pl.BlockSpec((1, d), const)]
        args += [proj, gnorm.reshape(1, d)]
    in_specs += [pl.BlockSpec(w_out.shape, const), pl.BlockSpec((TM, d), row),
                 pl.BlockSpec((1, 6, d), lambda i: (_mod_row(i, TM, segs), 0, 0)),
                 pl.BlockSpec((1, d), const), pl.BlockSpec((1, d), const),
                 pl.BlockSpec(rw_hi.shape, const), pl.BlockSpec(rw_lo.shape, const),
                 pl.BlockSpec(rb.shape, const), pl.BlockSpec((TM, TM), const)]
    args += [w_out, x, mod_l, ln_g.reshape(1, d), ln_b.reshape(1, d), rw_hi, rw_lo, rb, utri]
    return pl.pallas_call(
        functools.partial(_mix_out_kernel, rec=rec, alpha=alpha),
        grid=(nt // TM,),
        in_specs=in_specs,
        out_specs=[pl.BlockSpec((TM, d), row), pl.BlockSpec((ROUTE_ROWS, TM), lambda i: (0, i)),
                   pl.BlockSpec((N_EXPERTS, LANES), const)],
        out_shape=[jax.ShapeDtypeStruct((nt, d), F32), jax.ShapeDtypeStruct((ROUTE_ROWS, nt), F32),
                   jax.ShapeDtypeStruct((N_EXPERTS, LANES), F32)],
        scratch_shapes=[pltpu.VMEM((N_EXPERTS, LANES), F32)],
        compiler_params=_cparams(("arbitrary",)),
        name="mix_out_rec" if rec else "mix_out_attn",
    )(*args)


def _rows_to_tiles(ref, x):
    n = x.shape[0]
    for s in range(ROW_TILE):
        ref[pl.ds(s, n, stride=ROW_TILE), :] = x[:, s * LANES:(s + 1) * LANES]


def _tiles_to_rows(ref, n):
    return jnp.concatenate([ref[pl.ds(s, n, stride=ROW_TILE), :] for s in range(ROW_TILE)], axis=1)


def _dest_kernel(ps_ref, route_ref, o0_ref, o1_ref):
    r = route_ref[...]
    for k, o_ref in enumerate((o0_ref, o1_ref)):
        eid = r[k:k + 1].astype(jnp.int32)
        acc = jnp.zeros_like(eid)
        for e in range(N_EXPERTS):
            acc = jnp.where(eid == e, ps_ref[e], acc)
        o_ref[...] = (acc + r[2 + k:3 + k].astype(jnp.int32)) * ROW_TILE


def _dest(route, pad_starts):
    nt = route.shape[1]
    tb = 2048
    return pl.pallas_call(
        _dest_kernel,
        grid_spec=pltpu.PrefetchScalarGridSpec(
            num_scalar_prefetch=1,
            grid=(nt // tb,),
            in_specs=[pl.BlockSpec((ROUTE_ROWS, tb), lambda i, ps: (0, i))],
            out_specs=[pl.BlockSpec((1, tb), lambda i, ps: (0, i))] * 2,
        ),
        out_shape=[jax.ShapeDtypeStruct((1, nt), jnp.int32)] * 2,
        compiler_params=_cparams(("parallel",)),
        name="moe_dest",
    )(pad_starts, route)


def _dispatch_kernel(d0_ref, d1_ref, x_ref, mod_ref, xbuf_in_ref, xbuf_ref, h_ref, sem):
    del xbuf_in_ref
    i = pl.program_id(0)
    slot = lax.rem(i, 2)
    hs = h_ref.at[slot]
    m = mod_ref[0]
    _rows_to_tiles(hs, x_ref[...] * (1.0 + m[4:5]) + m[3:4])

    def body(t, carry):
        src = hs.at[pl.ds(pl.multiple_of(t * ROW_TILE, ROW_TILE), ROW_TILE)]
        for k, d_ref in enumerate((d0_ref, d1_ref)):
            dst = xbuf_ref.at[pl.ds(pl.multiple_of(d_ref[0, t], ROW_TILE), ROW_TILE)]
            pltpu.make_async_copy(src, dst, sem.at[slot, k]).start(priority=k)
        return carry

    lax.fori_loop(0, TM, body, 0, unroll=8)

    def drain(s):
        for k in range(2):
            pltpu.make_async_copy(h_ref.at[s], xbuf_ref.at[pl.ds(0, TM * ROW_TILE)], sem.at[s, k]).wait()

    @pl.when(i > 0)
    def _():
        drain(1 - slot)

    @pl.when(i == pl.num_programs(0) - 1)
    def _():
        drain(slot)


def _dispatch(x, mod_l, dest, xbuf, segs):
    nt, d = x.shape
    smem_row = pl.BlockSpec((1, TM), lambda i: (0, i), memory_space=pltpu.SMEM)
    return pl.pallas_call(
        _dispatch_kernel,
        grid=(nt // TM,),
        in_specs=[smem_row, smem_row,
                  pl.BlockSpec((TM, d), lambda i: (i, 0)),
                  pl.BlockSpec((1, 6, d), lambda i: (_mod_row(i, TM, segs), 0, 0)),
                  pl.BlockSpec(memory_space=pl.ANY)],
        out_specs=pl.BlockSpec(memory_space=pl.ANY),
        out_shape=jax.ShapeDtypeStruct(xbuf.shape, F32),
        scratch_shapes=[pltpu.VMEM((2, TM * ROW_TILE, LANES), F32), pltpu.SemaphoreType.DMA((2, 2))],
        input_output_aliases={4: 0},
        compiler_params=_cparams(("arbitrary",)),
        name="moe_dispatch",
    )(dest[0], dest[1], x, mod_l, xbuf)


def _expert_kernel(be_ref, nu_ref, x_ref, wg_ref, wu_ref, wd_ref, o_ref, wgb_ref, wub_ref, wdb_ref):
    i = pl.program_id(0)
    cur = jnp.minimum(i, nu_ref[0] - 1)
    new_expert = jnp.logical_or(i == 0, be_ref[cur] != be_ref[jnp.maximum(cur - 1, 0)])

    @pl.when(new_expert)
    def _():
        wgb_ref[...] = wg_ref[0, 0].astype(BF16)
        wub_ref[...] = wu_ref[0, 0].astype(BF16)
        wdb_ref[...] = wd_ref[0, 0].astype(BF16)

    @pl.when(i < nu_ref[0])
    def _():
        xb = _tiles_to_rows(x_ref, MOE_BLK).astype(BF16)
        hg = _dot(xb, wgb_ref[...])
        hu = _dot(xb, wub_ref[...])
        hid = (hg * jax.nn.sigmoid(hg) * hu).astype(BF16)
        _rows_to_tiles(o_ref, _dot(hid, wdb_ref[...]))

    @pl.when(i >= nu_ref[0])
    def _():
        o_ref[...] = jnp.zeros_like(o_ref)


def _experts(xbuf, blk_exp, n_used, wg, wu, wd, layer):
    n_blk = xbuf.shape[0] // (MOE_BLK * ROW_TILE)
    blk = lambda i, be, nu: (jnp.minimum(i, nu[0] - 1), 0)
    oblk = lambda i, be, nu: (i, 0)
    wmap = lambda i, be, nu: (layer, be[jnp.minimum(i, nu[0] - 1)], 0, 0)
    return pl.pallas_call(
        _expert_kernel,
        grid_spec=pltpu.PrefetchScalarGridSpec(
            num_scalar_prefetch=2,
            grid=(n_blk,),
            in_specs=[pl.BlockSpec((MOE_BLK * ROW_TILE, LANES), blk),
                      pl.BlockSpec((1, 1, D_MODEL, D_EXPERT), wmap),
                      pl.BlockSpec((1, 1, D_MODEL, D_EXPERT), wmap),
                      pl.BlockSpec((1, 1, D_EXPERT, D_MODEL), wmap)],
            out_specs=pl.BlockSpec((MOE_BLK * ROW_TILE, LANES), oblk),
            scratch_shapes=[pltpu.VMEM((D_MODEL, D_EXPERT), BF16), pltpu.VMEM((D_MODEL, D_EXPERT), BF16),
                            pltpu.VMEM((D_EXPERT, D_MODEL), BF16)],
        ),
        out_shape=jax.ShapeDtypeStruct(xbuf.shape, F32),
        compiler_params=_cparams(("arbitrary",)),
        name="moe_experts",
    )(blk_exp, n_used, xbuf, wg, wu, wd)


def _combine_kernel(d0_ref, d1_ref, n0_ref, n1_ref, ybuf_ref, x_ref, mod_ref, route_ref, sel_ref, lng_ref, lnb_ref,
                    *rest, alpha, split_blk):
    o_refs, (y_ref, sem) = rest[:-2], rest[-2:]
    i = pl.program_id(0)
    slot = lax.rem(i, 2)

    def gather(dk_refs, s):
        def body(t, carry):
            row = pl.ds(pl.multiple_of(t * ROW_TILE, ROW_TILE), ROW_TILE)
            for k, d_ref in enumerate(dk_refs):
                src = ybuf_ref.at[pl.ds(pl.multiple_of(d_ref[0, t], ROW_TILE), ROW_TILE)]
                pltpu.make_async_copy(src, y_ref.at[s, k, row], sem.at[s, k]).start(priority=k)
            return carry

        lax.fori_loop(0, TM, body, 0, unroll=8)

    @pl.when(i == 0)
    def _():
        gather((d0_ref, d1_ref), slot)

    @pl.when(i + 1 < pl.num_programs(0))
    def _():
        gather((n0_ref, n1_ref), 1 - slot)

    gates = _dot_tn(route_ref[...].astype(BF16), sel_ref[...])
    for k in range(2):
        pltpu.make_async_copy(ybuf_ref.at[pl.ds(0, TM * ROW_TILE)], y_ref.at[slot, k], sem.at[slot, k]).wait()
    m = mod_ref[0]
    g0 = jnp.concatenate([gates[:, 0:LANES]] * ROW_TILE, axis=1)
    g1 = jnp.concatenate([gates[:, LANES:]] * ROW_TILE, axis=1)
    y = _tiles_to_rows(y_ref.at[slot, 0], TM) * g0 + _tiles_to_rows(y_ref.at[slot, 1], TM) * g1
    out = _layer_norm(alpha * x_ref[...] + (1.0 + m[5:6]) * y, lng_ref[...], lnb_ref[...])
    if split_blk is None:
        o_refs[0][...] = out
    else:
        @pl.when(pl.program_id(0) < split_blk)
        def _():
            o_refs[0][...] = out

        @pl.when(pl.program_id(0) >= split_blk)
        def _():
            o_refs[1][...] = out


def _gate_select():
    sel = np.zeros((ROUTE_ROWS, 2 * LANES), np.float32)
    sel[4:7, 0:LANES] = 1.0
    sel[7:10, LANES:] = 1.0
    return jnp.asarray(sel, BF16)


def _combine(ybuf, dest, route, x, mod_l, ln_g, ln_b, segs, *, alpha, split=False):
    nt, d = x.shape
    const = lambda i: (0, 0)
    last = nt // TM - 1
    smem_row = pl.BlockSpec((1, TM), lambda i: (0, i), memory_space=pltpu.SMEM)
    smem_next = pl.BlockSpec((1, TM), lambda i: (0, jnp.minimum(i + 1, last)), memory_space=pltpu.SMEM)
    if split:
        n0 = segs[0][0] * segs[0][1]
        split_blk = n0 // TM
        out_specs = [pl.BlockSpec((TM, d), lambda i: (jnp.minimum(i, split_blk - 1), 0)),
                     pl.BlockSpec((TM, d), lambda i: (jnp.maximum(i - split_blk, 0), 0))]
        out_shape = [jax.ShapeDtypeStruct((n0, d), F32), jax.ShapeDtypeStruct((nt - n0, d), F32)]
    else:
        split_blk = None
        out_specs = [pl.BlockSpec((TM, d), lambda i: (i, 0))]
        out_shape = [jax.ShapeDtypeStruct((nt, d), F32)]
    res = pl.pallas_call(
        functools.partial(_combine_kernel, alpha=alpha, split_blk=split_blk),
        grid=(nt // TM,),
        in_specs=[smem_row, smem_row, smem_next, smem_next,
                  pl.BlockSpec(memory_space=pl.ANY),
                  pl.BlockSpec((TM, d), lambda i: (i, 0)),
                  pl.BlockSpec((1, 6, d), lambda i: (_mod_row(i, TM, segs), 0, 0)),
                  pl.BlockSpec((ROUTE_ROWS, TM), lambda i: (0, i)),
                  pl.BlockSpec((ROUTE_ROWS, 2 * LANES), const),
                  pl.BlockSpec((1, d), const), pl.BlockSpec((1, d), const)],
        out_specs=out_specs,
        out_shape=out_shape,
        scratch_shapes=[pltpu.VMEM((2, 2, TM * ROW_TILE, LANES), F32), pltpu.SemaphoreType.DMA((2, 2))],
        compiler_params=_cparams(("arbitrary",)),
        name="moe_combine",
    )(dest[0], dest[1], dest[0], dest[1], ybuf, x, mod_l, route, _gate_select(), ln_g.reshape(1, d),
      ln_b.reshape(1, d))
    return res if split else res[0]


def _moe_slots(nt):
    return 2 * nt + N_EXPERTS * MOE_BLK


def _moe(x, mod_l, route, counts, ln_g, ln_b, wg, wu, wd, layer, xbuf, segs, *, alpha, split=False):
    n_blk = _moe_slots(x.shape[0]) // MOE_BLK
    cnt = counts[:, 0].astype(jnp.int32)
    padded = (cnt + MOE_BLK - 1) // MOE_BLK * MOE_BLK
    pad_ends = jnp.cumsum(padded)
    pad_starts = (pad_ends - padded).astype(jnp.int32)
    blk_start = jnp.arange(n_blk, dtype=jnp.int32) * MOE_BLK
    blk_exp = jnp.minimum(jnp.sum(pad_ends[None, :] <= blk_start[:, None], axis=1), N_EXPERTS - 1).astype(jnp.int32)
    n_used = (pad_ends[-1:] // MOE_BLK).astype(jnp.int32)
    dest = _dest(route, pad_starts)
    xbuf = _dispatch(x, mod_l, dest, xbuf, segs)
    ybuf = _experts(xbuf, blk_exp, n_used, wg, wu, wd, layer)
    return _combine(ybuf, dest, route, x, mod_l, ln_g, ln_b, segs, alpha=alpha, split=split), xbuf


def _attn_w_in_layout(w_in):
    scale = HEAD_DIM ** -0.5
    qa = w_in[:, 0:512] * scale
    ka = w_in[:, 512:640]
    va = w_in[:, 640:768]
    qb = w_in[:, 768:1280] * scale
    kb = w_in[:, 1280:1792]
    vb = w_in[:, 1792:2304]
    dup = lambda a: jnp.concatenate([a[:, 0:64], a[:, 0:64], a[:, 64:128], a[:, 64:128]], axis=1)
    return jnp.concatenate([qa, dup(ka), dup(va), qb, kb, vb], axis=1).astype(BF16)


def _rec_w_in_layout(w_in):
    d = D_MODEL
    return jnp.concatenate([w_in[:, d:3 * d], w_in[:, 0:d], w_in[:, 3 * d:5 * d]], axis=1).astype(BF16)


def _rope_tables(t_max):
    inv = jnp.power(ROPE_THETA, -jnp.arange(0, HEAD_DIM, 2, dtype=F32) / HEAD_DIM)
    ang = jnp.arange(t_max, dtype=F32)[:, None] * inv[None, :]
    cos = jnp.cos(ang)
    sin = jnp.sin(ang)
    cos128 = jnp.concatenate([cos, cos, cos, cos], axis=1)
    sin128 = jnp.concatenate([-sin, sin, -sin, sin], axis=1)
    return cos128, sin128


def kernel(x_prompt, x_sample, c_prompt, c_sample, ada_w, ada_b, ln_g, ln_b, attn_w_in, attn_sink, nat_rel_bias, attn_w_out, rec_w_in, rec_lb, rec_gnorm, rec_w_out, router_w_group, router_b_group, router_w_expert, router_b_expert, expert_w_gate, expert_w_up, expert_w_down):
    depth = ada_w.shape[0]
    d = x_prompt.shape[-1]
    segs = ((x_prompt.shape[0], x_prompt.shape[1]), (x_sample.shape[0], x_sample.shape[1]))
    alpha = (2 * depth) ** 0.25
    x = jnp.concatenate([x_prompt.reshape(-1, d), x_sample.reshape(-1, d)], axis=0)
    c = jnp.concatenate([c_prompt, c_sample], axis=0)
    nseq = c.shape[0]
    mod = _adaln(c, ada_w, ada_b).reshape(depth, nseq, 6, d)
    rope_tables = _rope_tables(max(t for _, t in segs))
    xbuf = jnp.zeros((_moe_slots(x.shape[0]) * ROW_TILE, LANES), F32)
    for l in range(depth):
        mod_l = mod[l]
        i = l // 2
        router = _router_tables(router_w_group[l], router_b_group[l], router_w_expert[l], router_b_expert[l])
        if l % 2 == 0:
            proj = _inproj_attn(x, mod_l, _attn_w_in_layout(attn_w_in[i]), segs, rope_tables)
            oa = _win_attn(proj, attn_sink[i].astype(F32), segs)
            ob = _nat_attn(proj, _nat_bias_table(nat_rel_bias[i]), segs)
            x, route, counts = _mix_out((oa, ob), attn_w_out[i].astype(BF16), x, mod_l, ln_g[l, 0], ln_b[l, 0],
                                        router, segs, alpha=alpha)
        else:
            proj_z, proj_qvg = _inproj_rec(x, mod_l, _rec_w_in_layout(rec_w_in[i]), segs)
            o_f, o_b = _hgrn(proj_z, proj_qvg, rec_lb.astype(F32), i, segs)
            x, route, counts = _mix_out((o_f, o_b), rec_w_out[i].astype(BF16), x, mod_l, ln_g[l, 0], ln_b[l, 0],
                                        router, segs, alpha=alpha, rec_extra=(proj_qvg, rec_gnorm[i].astype(F32)))
        x, xbuf = _moe(x, mod_l, route, counts, ln_g[l, 1], ln_b[l, 1], expert_w_gate, expert_w_up, expert_w_down,
                       l, xbuf, segs, alpha=alpha, split=(l == depth - 1))
    return (x[0].reshape(x_prompt.shape), x[1].reshape(x_sample.shape))
```

```python
import functools

import numpy as np
import jax
import jax.numpy as jnp
from jax import lax
from jax.experimental import pallas as pl
from jax.experimental.pallas import tpu as pltpu

F32 = jnp.float32
BF16 = jnp.bfloat16
NEG_INF = float("-inf")

D_MODEL = 1024
HEAD_DIM = 64
A_HEADS = 8
A_KV_HEADS = 2
A_WINDOW = 128
A_BLOCK = 128
B_HEADS = 8
GRID_W = 64
NB_ROWS = 8
NB_COLS = 16
ROPE_THETA = 10000.0
C_HEADS = 8
C_KEY_DIM = 128
C_CHUNK = 64
N_GROUPS = 4
EXPERTS_PER_GROUP = 8
N_EXPERTS = 32
D_EXPERT = 512
LN_EPS = 1e-5
RMS_EPS = 1e-6
LOG2E = 1.4426950408889634

LANES = 128
ATTN_W = 2560
ROPE_W = 768
REC_W = 5 * D_MODEL
ROUTER_ROWS = 40
ROUTE_ROWS = 16
ROW_TILE = D_MODEL // LANES

TM = 512
TM_IN = 1024
MOE_BLK = 512
NAT_BLK = 512
REC_BLK = 1024
VMEM_LIMIT = 48 * 1024 * 1024


def _cparams(sem):
    return pltpu.CompilerParams(dimension_semantics=sem, vmem_limit_bytes=VMEM_LIMIT)


def _decode(i, blk, segs):
    out = None
    blk0 = 0
    for (b, t) in segs:
        nb = t // blk
        j = (i - blk0) % nb
        cand = (i - j, j, nb)
        out = cand if out is None else tuple(jnp.where(i < blk0, o, c) for o, c in zip(out, cand))
        blk0 += b * nb
    return out


def _mod_row(i, blk, segs):
    out = None
    blk0 = 0
    row0 = 0
    for (b, t) in segs:
        nb = t // blk
        cand = row0 + (i - blk0) // nb
        out = cand if out is None else jnp.where(i < blk0, out, cand)
        blk0 += b * nb
        row0 += b
    return out


def _pos_block(i, blk, segs):
    return _decode(i, blk, segs)[1]


def _dot(a, b):
    return jnp.dot(a, b, preferred_element_type=F32)


def _dot_nt(a, b):
    return lax.dot_general(a, b, (((1,), (1,)), ((), ())), preferred_element_type=F32)


def _dot_tn(a, b):
    return lax.dot_general(a, b, (((0,), (0,)), ((), ())), preferred_element_type=F32)


def _split2(x):
    hi = x.astype(BF16)
    lo = (x - hi.astype(F32)).astype(BF16)
    return hi, lo


def _layer_norm(x, g, b):
    mu = jnp.mean(x, axis=-1, keepdims=True)
    xc = x - mu
    var = jnp.mean(xc * xc, axis=-1, keepdims=True)
    return xc * lax.rsqrt(var + LN_EPS) * g + b


def _adaln_kernel(c_ref, w_ref, b_ref, o_ref):
    c = c_ref[...]
    cs = c * jax.nn.sigmoid(c)
    c_hi, c_lo = _split2(cs)
    w_hi, w_lo = _split2(w_ref[0])
    o_ref[0] = _dot(c_hi, w_hi) + _dot(c_hi, w_lo) + _dot(c_lo, w_hi) + b_ref[0]


def _adaln(c, ada_w, ada_b):
    depth, d, n = ada_w.shape
    nb = c.shape[0]
    tn = 1536
    return pl.pallas_call(
        _adaln_kernel,
        grid=(depth, n // tn),
        in_specs=[
            pl.BlockSpec((nb, d), lambda l, j: (0, 0)),
            pl.BlockSpec((1, d, tn), lambda l, j: (l, 0, j)),
            pl.BlockSpec((1, 1, tn), lambda l, j: (l, 0, j)),
        ],
        out_specs=pl.BlockSpec((1, nb, tn), lambda l, j: (l, 0, j)),
        out_shape=jax.ShapeDtypeStruct((depth, nb, n), F32),
        compiler_params=_cparams(("parallel", "parallel")),
        name="adaln",
    )(c, ada_w, ada_b.reshape(depth, 1, n))


def _rot_half_pairs(x):
    lane = lax.broadcasted_iota(jnp.int32, x.shape, 1)
    first = (lane & 63) < 32
    return jnp.where(first, pltpu.roll(x, 96, 1), pltpu.roll(x, 32, 1))


def _x_specs(xs, tm):
    d = xs[0].shape[1]
    if len(xs) == 1:
        return [pl.BlockSpec((tm, d), lambda i, *_: (i, 0))]
    nb0 = xs[0].shape[0] // tm
    return [pl.BlockSpec((tm, d), lambda i, *_: (jnp.minimum(i, nb0 - 1), 0)),
            pl.BlockSpec((tm, d), lambda i, *_: (jnp.maximum(i - nb0, 0), 0))]


def _x_block(x_refs, nb0):
    if len(x_refs) == 1:
        return x_refs[0][...]
    return jnp.where(pl.program_id(0) < nb0, x_refs[0][...], x_refs[1][...])


def _modulate_once(x_refs, nb0, mod_ref, xb_ref):
    @pl.when(pl.program_id(1) == 0)
    def _():
        m = mod_ref[0]
        xb_ref[...] = (_x_block(x_refs, nb0) * (1.0 + m[1:2]) + m[0:1]).astype(BF16)


def _inproj_attn_kernel(*refs, nb0):
    x_refs, (mod_ref, w_ref, cos_ref, sin_ref, o_ref, xb_ref) = refs[:-6], refs[-6:]
    _modulate_once(x_refs, nb0, mod_ref, xb_ref)
    y = _dot(xb_ref[...], w_ref[...])

    @pl.when(pl.program_id(1) == 0)
    def _():
        cos = cos_ref[...]
        sin = sin_ref[...]
        for g in range(ROPE_W // LANES):
            blk = y[:, g * LANES:(g + 1) * LANES]
            o_ref[:, g * LANES:(g + 1) * LANES] = (blk * cos + _rot_half_pairs(blk) * sin).astype(o_ref.dtype)
        o_ref[:, ROPE_W:] = y[:, ROPE_W:].astype(o_ref.dtype)

    @pl.when(pl.program_id(1) != 0)
    def _():
        o_ref[...] = y.astype(o_ref.dtype)


def _inproj_rec_kernel(x_ref, mod_ref, w_ref, oz_ref, oq_ref, xb_ref, *, nz):
    _modulate_once((x_ref,), None, mod_ref, xb_ref)
    y = _dot(xb_ref[...], w_ref[...])

    @pl.when(pl.program_id(1) < nz)
    def _():
        oz_ref[...] = y

    @pl.when(pl.program_id(1) >= nz)
    def _():
        oq_ref[...] = y.astype(oq_ref.dtype)


def _inproj_specs(xs, segs, tn):
    d = xs[0].shape[1]
    return _x_specs(xs, TM_IN) + [pl.BlockSpec((1, 6, d), lambda i, j: (_mod_row(i, TM_IN, segs), 0, 0)),
                                  pl.BlockSpec((d, tn), lambda i, j: (0, j))]


def _inproj_attn(xs, mod_l, w, segs, rope_tables):
    nt = sum(x.shape[0] for x in xs)
    d = xs[0].shape[1]
    n = w.shape[1]
    tn = n // 2
    tab = pl.BlockSpec((TM_IN, LANES), lambda i, j: (_pos_block(i, TM_IN, segs), 0))
    return pl.pallas_call(
        functools.partial(_inproj_attn_kernel, nb0=xs[0].shape[0] // TM_IN),
        grid=(nt // TM_IN, n // tn),
        in_specs=_inproj_specs(xs, segs, tn) + [tab, tab],
        out_specs=pl.BlockSpec((TM_IN, tn), lambda i, j: (i, j)),
        out_shape=jax.ShapeDtypeStruct((nt, n), BF16),
        scratch_shapes=[pltpu.VMEM((TM_IN, d), BF16)],
        compiler_params=_cparams(("parallel", "arbitrary")),
        name="inproj_rope",
    )(*xs, mod_l, w, *rope_tables)


def _inproj_rec(x, mod_l, w, segs):
    nt, d = x.shape
    tn = D_MODEL
    nz = 2
    nq = w.shape[1] // tn - nz
    return pl.pallas_call(
        functools.partial(_inproj_rec_kernel, nz=nz),
        grid=(nt // TM_IN, nz + nq),
        in_specs=_inproj_specs((x,), segs, tn),
        out_specs=[pl.BlockSpec((TM_IN, tn), lambda i, j: (i, jnp.minimum(j, nz - 1))),
                   pl.BlockSpec((TM_IN, tn), lambda i, j: (i, jnp.maximum(j - nz, 0)))],
        out_shape=[jax.ShapeDtypeStruct((nt, nz * tn), F32), jax.ShapeDtypeStruct((nt, nq * tn), BF16)],
        scratch_shapes=[pltpu.VMEM((TM_IN, d), BF16)],
        compiler_params=_cparams(("parallel", "arbitrary")),
        name="inproj_rec",
    )(x, mod_l, w)


def _win_attn_kernel(sink_ref, q_ref, k0_ref, k1_ref, k2_ref, v0_ref, v1_ref, v2_ref, o_ref, *, segs):
    i = pl.program_id(0)
    _, j, nb = _decode(i, A_BLOCK, segs)
    ws = jnp.clip(j - 1, 0, nb - 3)
    span = 3 * A_BLOCK
    qpos = j * A_BLOCK + lax.broadcasted_iota(jnp.int32, (A_BLOCK, span), 0)
    kpos = ws * A_BLOCK + lax.broadcasted_iota(jnp.int32, (A_BLOCK, span), 1)
    valid = jnp.abs(qpos - kpos) <= A_WINDOW
    lo = lax.broadcasted_iota(jnp.int32, (A_BLOCK, LANES), 1) < HEAD_DIM
    kfull = jnp.concatenate([k0_ref[...], k1_ref[...], k2_ref[...]], axis=0)
    vfull = jnp.concatenate([v0_ref[...], v1_ref[...], v2_ref[...]], axis=0)
    zero = jnp.zeros((A_BLOCK, LANES), BF16)
    grp = A_HEADS // A_KV_HEADS
    for kvh in range(A_KV_HEADS):
        kk = kfull[:, kvh * LANES:(kvh + 1) * LANES]
        vv = vfull[:, kvh * LANES:(kvh + 1) * LANES]
        parts = []
        for p in range(grp // 2):
            c0 = (kvh * (grp // 2) + p) * LANES
            qp = q_ref[:, c0:c0 + LANES]
            parts.append(jnp.where(lo, qp, zero))
            parts.append(jnp.where(lo, zero, qp))
        qs = jnp.concatenate(parts, axis=0)
        s = _dot_nt(qs, kk)
        probs = []
        inv = []
        for h in range(grp):
            sh = jnp.where(valid, s[h * A_BLOCK:(h + 1) * A_BLOCK], NEG_INF)
            sink = sink_ref[kvh * grp + h]
            m = jnp.maximum(jnp.max(sh, axis=-1, keepdims=True), sink)
            e = jnp.exp(sh - m)
            den = jnp.sum(e, axis=-1, keepdims=True) + jnp.exp(sink - m)
            probs.append(e.astype(BF16))
            inv.append(1.0 / den)
        o = _dot(jnp.concatenate(probs, axis=0), vv)
        for p in range(grp // 2):
            oe = o[(2 * p) * A_BLOCK:(2 * p + 1) * A_BLOCK] * inv[2 * p]
            oo = o[(2 * p + 1) * A_BLOCK:(2 * p + 2) * A_BLOCK] * inv[2 * p + 1]
            c0 = (kvh * (grp // 2) + p) * LANES
            o_ref[:, c0:c0 + LANES] = jnp.where(lo, oe, oo).astype(o_ref.dtype)


def _win_attn(proj, sink, segs):
    nt = proj.shape[0]
    kvw = 2 * LANES

    def kv_map(d, col):
        def f(i):
            base, j, nb = _decode(i, A_BLOCK, segs)
            return (base + jnp.clip(j - 1, 0, nb - 3) + d, col)
        return f

    in_specs = [pl.BlockSpec(memory_space=pltpu.SMEM),
                pl.BlockSpec((A_BLOCK, 512), lambda i: (i, 0))]
    in_specs += [pl.BlockSpec((A_BLOCK, kvw), kv_map(d, 2)) for d in range(3)]
    in_specs += [pl.BlockSpec((A_BLOCK, kvw), kv_map(d, 3)) for d in range(3)]
    return pl.pallas_call(
        functools.partial(_win_attn_kernel, segs=segs),
        grid=(nt // A_BLOCK,),
        in_specs=in_specs,
        out_specs=pl.BlockSpec((A_BLOCK, 512), lambda i: (i, 0)),
        out_shape=jax.ShapeDtypeStruct((nt, 512), BF16),
        compiler_params=_cparams(("parallel",)),
        name="win_attn",
    )(sink, proj, proj, proj, proj, proj, proj, proj)


def _nat_bias_table(rel_bias):
    c = np.arange(GRID_W)
    kc = np.arange(GRID_W)
    qwin = np.clip(c - NB_COLS // 2, 0, GRID_W - NB_COLS)
    ok = (kc[None, :] >= qwin[:, None]) & (kc[None, :] < qwin[:, None] + NB_COLS)
    dc = np.clip(kc[None, :] - c[:, None], -(NB_COLS - 1), NB_COLS - 1) + NB_COLS - 1
    var = np.arange(NB_ROWS)
    irow = np.arange(NB_ROWS)
    dr = irow[None, :] - var[:, None] + NB_ROWS - 1
    t = rel_bias.astype(F32)[:, dr][:, :, :, dc]
    t = jnp.where(jnp.asarray(ok)[None, None, None], t, NEG_INF)
    t = t.transpose(1, 0, 3, 2, 4)
    return t.reshape(NB_ROWS, B_HEADS, GRID_W, NB_ROWS * GRID_W)


def _nat_attn_kernel(q_ref, k0, k1, k2, k3, v0, v1, v2, v3, bias_ref, o_ref, kbuf, vbuf, *, segs):
    i = pl.program_id(0)
    _, j, nb = _decode(i, NAT_BLK, segs)
    rows = nb * NB_ROWS
    w0 = jnp.clip(NB_ROWS * j - NB_ROWS // 2, 0, rows - 2 * NB_ROWS)
    for d, (kr, vr) in enumerate(((k0, v0), (k1, v1), (k2, v2), (k3, v3))):
        kbuf[d * 256:(d + 1) * 256, :] = kr[...]
        vbuf[d * 256:(d + 1) * 256, :] = vr[...]
    lo = lax.broadcasted_iota(jnp.int32, (GRID_W, LANES), 1) < HEAD_DIM
    zero = jnp.zeros((GRID_W, LANES), BF16)
    nkeys = NB_ROWS * GRID_W

    def scores(rr, hp):
        r = NB_ROWS * j + rr
        rs = jnp.clip(r - NB_ROWS // 2, 0, rows - NB_ROWS)
        off = pl.multiple_of((rs - w0) * GRID_W, GRID_W)
        cs = slice(hp * LANES, (hp + 1) * LANES)
        qp = q_ref[rr * GRID_W:(rr + 1) * GRID_W, cs]
        qs = jnp.concatenate([jnp.where(lo, qp, zero), jnp.where(lo, zero, qp)], axis=0)
        s = _dot_nt(qs, kbuf[pl.ds(off, nkeys), cs])
        return s, r - rs, off

    units = [(rr, hp) for rr in range(NB_ROWS) for hp in range(B_HEADS // 2)]
    nxt = scores(*units[0])
    for u, (rr, hp) in enumerate(units):
        s, var, off = nxt
        if u + 1 < len(units):
            nxt = scores(*units[u + 1])
        cs = slice(hp * LANES, (hp + 1) * LANES)
        s = s + jnp.concatenate([bias_ref[var, 2 * hp], bias_ref[var, 2 * hp + 1]], axis=0)
        m = jnp.max(s, axis=-1, keepdims=True)
        e = jnp.exp(s - m)
        inv = 1.0 / jnp.sum(e, axis=-1, keepdims=True)
        o = _dot(e.astype(BF16), vbuf[pl.ds(off, nkeys), cs]) * inv
        o_ref[rr * GRID_W:(rr + 1) * GRID_W, cs] = jnp.where(lo, o[:GRID_W], o[GRID_W:]).astype(o_ref.dtype)


def _nat_attn(proj, bias_table, segs):
    nt = proj.shape[0]

    def kv_map(d, col):
        def f(i):
            base, j, nb = _decode(i, NAT_BLK, segs)
            return (2 * base + jnp.clip(2 * j - 1, 0, 2 * nb - 4) + d, col)
        return f

    in_specs = [pl.BlockSpec((NAT_BLK, 512), lambda i: (i, 2))]
    in_specs += [pl.BlockSpec((256, 512), kv_map(d, 3)) for d in range(4)]
    in_specs += [pl.BlockSpec((256, 512), kv_map(d, 4)) for d in range(4)]
    in_specs += [pl.BlockSpec(bias_table.shape, lambda i: (0, 0, 0, 0))]
    return pl.pallas_call(
        functools.partial(_nat_attn_kernel, segs=segs),
        grid=(nt // NAT_BLK,),
        in_specs=in_specs,
        out_specs=pl.BlockSpec((NAT_BLK, 512), lambda i: (i, 0)),
        out_shape=jax.ShapeDtypeStruct((nt, 512), BF16),
        scratch_shapes=[pltpu.VMEM((1024, 512), BF16), pltpu.VMEM((1024, 512), BF16)],
        compiler_params=_cparams(("parallel",)),
        name="nat_attn",
    )(*([proj] * 9), bias_table)


def _chunk_cumsum(x, fwd):
    n = x.shape[0]
    x3 = x.reshape(n // 8, 8, LANES)
    sub = lax.broadcasted_iota(jnp.int32, x3.shape, 1)
    s = 1
    while s < 8:
        if fwd:
            x3 = x3 + jnp.where(sub >= s, pltpu.roll(x3, s, 1), 0.0)
        else:
            x3 = x3 + jnp.where(sub < 8 - s, pltpu.roll(x3, 8 - s, 1), 0.0)
        s *= 2
    nv = n // 8
    outs = [None] * nv
    edges = [None] * nv
    run = None
    edge = 7 if fwd else 0
    for j in (range(nv) if fwd else range(nv - 1, -1, -1)):
        cur = x3[j]
        if run is not None:
            cur = cur + run
        outs[j] = cur
        run = jnp.broadcast_to(cur[edge:edge + 1, :], (8, LANES))
        edges[j] = run
    return outs, edges


def _split_rows(outs, edges, m, fwd):
    nv = len(outs)
    if m >= 8:
        g = m // 8
        res = []
        for j in range(nv):
            first = (j // (2 * g)) * 2 * g
            res.append(edges[first + g - 1] if fwd else edges[first + g])
        return res
    sub = lax.broadcasted_iota(jnp.int32, (8, LANES), 0)
    pick = m - 1 if fwd else m
    res = []
    for j in range(nv):
        r = None
        for gi in reversed(range(8 // (2 * m))):
            row = gi * 2 * m + pick
            bc = jnp.broadcast_to(outs[j][row:row + 1, :], (8, LANES))
            r = bc if r is None else jnp.where(sub < (gi + 1) * 2 * m, bc, r)
        res.append(r)
    return res


def _hgrn_chunk(q, z, vb, lower, masks, fwd):
    n = C_CHUNK
    f = lower + (1.0 - lower) * jax.nn.sigmoid(z)
    kk = 1.0 - f
    outs, edges = _chunk_cumsum(jnp.log2(f), fwd)
    b = jnp.concatenate(outs, axis=0)
    total = b[n - 1:n] if fwd else b[0:1]
    rowi = lax.broadcasted_iota(jnp.int32, (n, 1), 0)
    zero = jnp.zeros((n, LANES), BF16)
    xs = []
    m = n // 2
    while m >= 2:
        r = jnp.concatenate(_split_rows(outs, edges, m, fwd), axis=0)
        upper = (rowi & (2 * m - 1)) >= m
        qside = upper if fwd else jnp.logical_not(upper)
        e = jnp.exp2((b - r) * masks[N_PAIR_MASKS + len(xs)])
        xs.append((jnp.where(qside, q, kk) * e).astype(BF16))
        m //= 2
    odd = (rowi & 1) == 1
    qside = odd if fwd else jnp.logical_not(odd)
    xs.append(jnp.where(qside, q * f, kk).astype(BF16))
    ti = lax.broadcasted_iota(jnp.int32, (n, LANES), 0)
    li = lax.broadcasted_iota(jnp.int32, (n, LANES), 1)
    asum = jnp.where(ti == li, jnp.sum(q * kk, axis=-1, keepdims=True), 0.0)
    for p in range(len(xs) // 2):
        x1, x2 = xs[2 * p], xs[2 * p + 1]
        lhs = jnp.concatenate([x1, x2], axis=1)
        rhs = jnp.concatenate([jnp.concatenate([x1, zero], axis=1), jnp.concatenate([zero, x2], axis=1)], axis=0)
        asum = asum + _dot_nt(lhs, rhs) * masks[p]
    ke = (kk * jnp.exp2(total - b)).astype(BF16)
    u = _dot_tn(vb, ke)
    o = _dot(asum.astype(BF16), jnp.concatenate([vb, vb], axis=0))
    qe = (q * jnp.exp2(b)).astype(BF16)
    return o, qe, u, jnp.exp2(total)


N_PAIR_MASKS = 3


def _level_masks(fwd):
    n = C_CHUNK
    t = np.arange(n)[:, None]
    s = np.arange(n)[None, :]
    out = np.zeros((N_PAIR_MASKS + 5, n, 2 * n), np.float32)
    for i, m in enumerate([32, 16, 8, 4, 2, 1]):
        same = (t // (2 * m)) == (s // (2 * m))
        tu = (t % (2 * m)) >= m
        su = (s % (2 * m)) >= m
        ok = same & ((tu & ~su) if fwd else (~tu & su))
        out[i // 2, :, (i % 2) * n:(i % 2 + 1) * n] = ok
        if m >= 2:
            out[N_PAIR_MASKS + i] = np.where(tu if fwd else ~tu, 1.0, -1.0)
    return out


def _hgrn_kernel(lb_ref, mf_ref, mb_ref, qf_ref, zf_ref, vf_ref, qb_ref, zb_ref, vb_ref,
                 of_ref, ob_ref, sf_ref, sb_ref, *, segs, layer):
    i = pl.program_id(1)
    _, j, _ = _decode(i, REC_BLK, segs)

    @pl.when(j == 0)
    def _():
        sf_ref[...] = jnp.zeros_like(sf_ref)
        sb_ref[...] = jnp.zeros_like(sb_ref)

    lb = lb_ref[...]
    ex = jnp.exp(lb - jnp.max(lb, axis=0, keepdims=True))
    p = ex / jnp.sum(ex, axis=0, keepdims=True)
    lower = jnp.sum(p[0:layer + 1], axis=0, keepdims=True) - p[0:1]
    mf, mb = mf_ref, mb_ref
    nch = REC_BLK // C_CHUNK
    dirs = ((qf_ref, zf_ref, vf_ref, of_ref, sf_ref, mf, True), (qb_ref, zb_ref, vb_ref, ob_ref, sb_ref, mb, False))
    parts = ([], [])
    for c in range(nch):
        for d, (q_ref, z_ref, v_ref, _, _, masks, fwd) in enumerate(dirs):
            cc = c if fwd else nch - 1 - c
            sl = slice(cc * C_CHUNK, (cc + 1) * C_CHUNK)
            parts[d].append((sl,) + _hgrn_chunk(q_ref[sl, :].astype(F32), z_ref[sl, :], v_ref[sl, :], lower, masks, fwd))
    for d, (_, _, _, o_ref, s_ref, _, _) in enumerate(dirs):
        st = s_ref[...]
        for (sl, o, qe, u, dec) in parts[d]:
            o_ref[sl, :] = o + _dot_nt(qe, st.astype(BF16))
            st = st * dec + u
        s_ref[...] = st


def _hgrn(proj_z, proj_qvg, rec_lb, layer, segs):
    nt = proj_z.shape[0]
    nrec = rec_lb.shape[0]
    mf = jnp.asarray(_level_masks(True))
    mb = jnp.asarray(_level_masks(False))

    def fwd_map(col):
        return lambda h, i: (i, col * C_HEADS + h)

    def bwd_map(col):
        def f(h, i):
            base, j, nb = _decode(i, REC_BLK, segs)
            return (base + nb - 1 - j, col * C_HEADS + h)
        return f

    blk = (REC_BLK, C_KEY_DIM)
    c3 = lambda h, i: (0, 0, 0)
    in_specs = [pl.BlockSpec((nrec, C_KEY_DIM), lambda h, i: (0, h)),
                pl.BlockSpec(mf.shape, c3), pl.BlockSpec(mb.shape, c3),
                pl.BlockSpec(blk, fwd_map(0)), pl.BlockSpec(blk, fwd_map(0)), pl.BlockSpec(blk, fwd_map(1)),
                pl.BlockSpec(blk, bwd_map(0)), pl.BlockSpec(blk, bwd_map(1)), pl.BlockSpec(blk, bwd_map(1))]
    out_shape = [jax.ShapeDtypeStruct((nt, D_MODEL), F32)] * 2
    return pl.pallas_call(
        functools.partial(_hgrn_kernel, segs=segs, layer=layer),
        grid=(C_HEADS, nt // REC_BLK),
        in_specs=in_specs,
        out_specs=[pl.BlockSpec(blk, fwd_map(0)), pl.BlockSpec(blk, bwd_map(0))],
        out_shape=out_shape,
        scratch_shapes=[pltpu.VMEM((C_KEY_DIM, C_KEY_DIM), F32)] * 2,
        compiler_params=_cparams(("parallel", "arbitrary")),
        name="hgrn",
    )(rec_lb, mf, mb, proj_qvg, proj_z, proj_qvg, proj_qvg, proj_z, proj_qvg)


def _route_rows(logits_t, utri, cnt_ref):
    tm = logits_t.shape[1]
    gi = lax.broadcasted_iota(jnp.int32, (8, tm), 0).astype(F32)
    lg = jnp.where(gi < N_GROUPS, logits_t[0:8], NEG_INF)
    gm = jnp.max(lg, axis=0, keepdims=True)
    p_grp = 1.0 / jnp.sum(jnp.exp(lg - gm), axis=0, keepdims=True)
    grp = jnp.min(jnp.where(lg == gm, gi, 99.0), axis=0, keepdims=True)
    ei = lax.broadcasted_iota(jnp.int32, (N_EXPERTS, tm), 0).astype(F32)
    egrp = (lax.broadcasted_iota(jnp.int32, (N_EXPERTS, tm), 0) >> 3).astype(F32)
    le = jnp.where(egrp == grp, logits_t[8:8 + N_EXPERTS], NEG_INF)
    v1 = jnp.max(le, axis=0, keepdims=True)
    i1 = jnp.min(jnp.where(le == v1, ei, 99.0), axis=0, keepdims=True)
    le2 = jnp.where(ei == i1, NEG_INF, le)
    v2 = jnp.max(le2, axis=0, keepdims=True)
    i2 = jnp.min(jnp.where(le2 == v2, ei, 99.0), axis=0, keepdims=True)
    t = jnp.exp(v2 - v1)
    g1 = p_grp / (1.0 + t)
    g2 = p_grp * t / (1.0 + t)
    o1 = jnp.where(ei == i1, 1.0, 0.0)
    o2 = jnp.where(ei == i2, 1.0, 0.0)
    osum = o1 + o2
    before = _dot(osum.astype(BF16), utri) + cnt_ref[:, 0:1]
    rank1 = jnp.sum(o1 * before, axis=0, keepdims=True)
    rank2 = jnp.sum(o2 * before, axis=0, keepdims=True)
    cnt_ref[...] = cnt_ref[...] + jnp.sum(osum, axis=1, keepdims=True)

    def pieces(g):
        hi = g.astype(BF16).astype(F32)
        r = g - hi
        mid = r.astype(BF16).astype(F32)
        return [hi, mid, r - mid]

    rows = [i1, i2, rank1, rank2] + pieces(g1) + pieces(g2)
    return rows + [jnp.zeros((1, tm), F32)] * (ROUTE_ROWS - len(rows))


def _mix_out_kernel(*refs, rec, alpha, n_x, nb0):
    x_refs, refs = refs[:n_x], refs[n_x:]
    if rec:
        (a_ref, b_ref, g_ref, gn_ref, w_ref, mod_ref, lng_ref, lnb_ref,
         rw_hi_ref, rw_lo_ref, rb_ref, utri_ref, xo_ref, route_ref, cnt_out_ref, cnt_ref) = refs
    else:
        (a_ref, b_ref, w_ref, mod_ref, lng_ref, lnb_ref,
         rw_hi_ref, rw_lo_ref, rb_ref, utri_ref, xo_ref, route_ref, cnt_out_ref, cnt_ref) = refs
    i = pl.program_id(0)

    @pl.when(i == 0)
    def _():
        cnt_ref[...] = jnp.zeros_like(cnt_ref)

    if rec:
        o = a_ref[...] + b_ref[...]
        g = g_ref[...].astype(F32)
        gate = gn_ref[...] * (g * jax.nn.sigmoid(g))
        parts = []
        for h in range(C_HEADS):
            blk = o[:, h * C_KEY_DIM:(h + 1) * C_KEY_DIM]
            ms = jnp.mean(blk * blk, axis=-1, keepdims=True)
            parts.append(blk * lax.rsqrt(ms + RMS_EPS))
        lhs = (jnp.concatenate(parts, axis=1) * gate).astype(BF16)
        y = _dot(lhs, w_ref[...])
    else:
        half = a_ref.shape[1]
        y = _dot(a_ref[...], w_ref[0:half, :]) + _dot(b_ref[...], w_ref[half:, :])
    m = mod_ref[0]
    xn = _layer_norm(alpha * _x_block(x_refs, nb0) + (1.0 + m[2:3]) * y, lng_ref[...], lnb_ref[...])
    xo_ref[...] = xn
    h_hi, h_lo = _split2(xn * (1.0 + m[4:5]) + m[3:4])
    rw_hi = rw_hi_ref[...]
    logits_t = _dot_nt(rw_hi, h_hi) + _dot_nt(rw_hi, h_lo) + _dot_nt(rw_lo_ref[...], h_hi) + rb_ref[:, 0:1]
    for k, row in enumerate(_route_rows(logits_t, utri_ref[...], cnt_ref)):
        route_ref[k:k + 1, :] = row
    cnt_out_ref[...] = cnt_ref[...]


def _router_tables(w_rg, b_rg, w_re, b_re):
    d = w_rg.shape[0]
    wt = jnp.zeros((ROUTER_ROWS, d), F32)
    wt = wt.at[0:N_GROUPS].set(w_rg.astype(F32).T).at[8:8 + N_EXPERTS].set(w_re.astype(F32).T)
    hi = wt.astype(BF16)
    lo = (wt - hi.astype(F32)).astype(BF16)
    rb = jnp.zeros((ROUTER_ROWS,), F32).at[0:N_GROUPS].set(b_rg.astype(F32))
    rb = rb.at[8:8 + N_EXPERTS].set(b_re.astype(F32).reshape(-1))
    return hi, lo, jnp.broadcast_to(rb[:, None], (ROUTER_ROWS, LANES))


def _mix_out(lhs, w_out, xs, mod_l, ln_g, ln_b, router, segs, *, alpha, rec_extra=None):
    nt = sum(x.shape[0] for x in xs)
    d = xs[0].shape[1]
    rec = rec_extra is not None
    rw_hi, rw_lo, rb = router
    idx = np.arange(TM)
    utri = jnp.asarray(idx[:, None] < idx[None, :], BF16)
    row = lambda i: (i, 0)
    const = lambda i: (0, 0)
    a, b = lhs
    in_specs = _x_specs(xs, TM) + [pl.BlockSpec((TM, a.shape[1]), row), pl.BlockSpec((TM, b.shape[1]), row)]
    args = [*xs, a, b]
    if rec:
        proj, gnorm = rec_extra
        in_specs += [pl.BlockSpec((TM, d), lambda i: (i, 2)), pl.BlockSpec((1, d), const)]
        args += [proj, gnorm.reshape(1, d)]
    in_specs += [pl.BlockSpec(w_out.shape, const),
                 pl.BlockSpec((1, 6, d), lambda i: (_mod_row(i, TM, segs), 0, 0)),
                 pl.BlockSpec((1, d), const), pl.BlockSpec((1, d), const),
                 pl.BlockSpec(rw_hi.shape, const), pl.BlockSpec(rw_lo.shape, const),
                 pl.BlockSpec(rb.shape, const), pl.BlockSpec((TM, TM), const)]
    args += [w_out, mod_l, ln_g.reshape(1, d), ln_b.reshape(1, d), rw_hi, rw_lo, rb, utri]
    return pl.pallas_call(
        functools.partial(_mix_out_kernel, rec=rec, alpha=alpha, n_x=len(xs), nb0=xs[0].shape[0] // TM),
        grid=(nt // TM,),
        in_specs=in_specs,
        out_specs=[pl.BlockSpec((TM, d), row), pl.BlockSpec((ROUTE_ROWS, TM), lambda i: (0, i)),
                   pl.BlockSpec((N_EXPERTS, LANES), const)],
        out_shape=[jax.ShapeDtypeStruct((nt, d), F32), jax.ShapeDtypeStruct((ROUTE_ROWS, nt), F32),
                   jax.ShapeDtypeStruct((N_EXPERTS, LANES), F32)],
        scratch_shapes=[pltpu.VMEM((N_EXPERTS, LANES), F32)],
        compiler_params=_cparams(("arbitrary",)),
        name="mix_out_rec" if rec else "mix_out_attn",
    )(*args)


def _rows_to_tiles(ref, x):
    n = x.shape[0]
    for s in range(ROW_TILE):
        ref[pl.ds(s, n, stride=ROW_TILE), :] = x[:, s * LANES:(s + 1) * LANES]


def _tiles_to_rows(ref, n):
    return jnp.concatenate([ref[pl.ds(s, n, stride=ROW_TILE), :] for s in range(ROW_TILE)], axis=1)


def _dest_kernel(ps_ref, route_ref, o0_ref, o1_ref):
    r = route_ref[...]
    for k, o_ref in enumerate((o0_ref, o1_ref)):
        eid = r[k:k + 1].astype(jnp.int32)
        acc = jnp.zeros_like(eid)
        for e in range(N_EXPERTS):
            acc = jnp.where(eid == e, ps_ref[e], acc)
        o_ref[...] = (acc + r[2 + k:3 + k].astype(jnp.int32)) * ROW_TILE


def _dest(route, pad_starts):
    nt = route.shape[1]
    tb = 2048
    return pl.pallas_call(
        _dest_kernel,
        grid_spec=pltpu.PrefetchScalarGridSpec(
            num_scalar_prefetch=1,
            grid=(nt // tb,),
            in_specs=[pl.BlockSpec((ROUTE_ROWS, tb), lambda i, ps: (0, i))],
            out_specs=[pl.BlockSpec((1, tb), lambda i, ps: (0, i))] * 2,
        ),
        out_shape=[jax.ShapeDtypeStruct((1, nt), jnp.int32)] * 2,
        compiler_params=_cparams(("parallel",)),
        name="moe_dest",
    )(pad_starts, route)


def _dispatch_kernel(d0_ref, d1_ref, x_ref, mod_ref, xbuf_in_ref, xbuf_ref, h_ref, sem):
    del xbuf_in_ref
    i = pl.program_id(0)
    slot = lax.rem(i, 2)
    hs = h_ref.at[slot]
    m = mod_ref[0]
    _rows_to_tiles(hs, x_ref[...] * (1.0 + m[4:5]) + m[3:4])

    def body(t, carry):
        src = hs.at[pl.ds(pl.multiple_of(t * ROW_TILE, ROW_TILE), ROW_TILE)]
        for k, d_ref in enumerate((d0_ref, d1_ref)):
            dst = xbuf_ref.at[pl.ds(pl.multiple_of(d_ref[0, t], ROW_TILE), ROW_TILE)]
            pltpu.make_async_copy(src, dst, sem.at[slot, k]).start(priority=k)
        return carry

    lax.fori_loop(0, TM, body, 0, unroll=8)

    def drain(s):
        for k in range(2):
            pltpu.make_async_copy(h_ref.at[s], xbuf_ref.at[pl.ds(0, TM * ROW_TILE)], sem.at[s, k]).wait()

    @pl.when(i > 0)
    def _():
        drain(1 - slot)

    @pl.when(i == pl.num_programs(0) - 1)
    def _():
        drain(slot)


def _dispatch(x, mod_l, dest, xbuf, segs):
    nt, d = x.shape
    smem_row = pl.BlockSpec((1, TM), lambda i: (0, i), memory_space=pltpu.SMEM)
    return pl.pallas_call(
        _dispatch_kernel,
        grid=(nt // TM,),
        in_specs=[smem_row, smem_row,
                  pl.BlockSpec((TM, d), lambda i: (i, 0)),
                  pl.BlockSpec((1, 6, d), lambda i: (_mod_row(i, TM, segs), 0, 0)),
                  pl.BlockSpec(memory_space=pl.ANY)],
        out_specs=pl.BlockSpec(memory_space=pl.ANY),
        out_shape=jax.ShapeDtypeStruct(xbuf.shape, F32),
        scratch_shapes=[pltpu.VMEM((2, TM * ROW_TILE, LANES), F32), pltpu.SemaphoreType.DMA((2, 2))],
        input_output_aliases={4: 0},
        compiler_params=_cparams(("arbitrary",)),
        name="moe_dispatch",
    )(dest[0], dest[1], x, mod_l, xbuf)


def _expert_kernel(be_ref, nu_ref, x_ref, wg_ref, wu_ref, wd_ref, o_ref, wgb_ref, wub_ref, wdb_ref):
    i = pl.program_id(0)
    cur = jnp.minimum(i, nu_ref[0] - 1)
    new_expert = jnp.logical_or(i == 0, be_ref[cur] != be_ref[jnp.maximum(cur - 1, 0)])

    @pl.when(new_expert)
    def _():
        wgb_ref[...] = wg_ref[0, 0].astype(BF16)
        wub_ref[...] = wu_ref[0, 0].astype(BF16)
        wdb_ref[...] = wd_ref[0, 0].astype(BF16)

    @pl.when(i < nu_ref[0])
    def _():
        xb = _tiles_to_rows(x_ref, MOE_BLK).astype(BF16)
        hg = _dot(xb, wgb_ref[...])
        hu = _dot(xb, wub_ref[...])
        hid = (hg * jax.nn.sigmoid(hg) * hu).astype(BF16)
        _rows_to_tiles(o_ref, _dot(hid, wdb_ref[...]))

    @pl.when(i >= nu_ref[0])
    def _():
        o_ref[...] = jnp.zeros_like(o_ref)


def _experts(xbuf, blk_exp, n_used, wg, wu, wd, layer):
    n_blk = xbuf.shape[0] // (MOE_BLK * ROW_TILE)
    blk = lambda i, be, nu: (jnp.minimum(i, nu[0] - 1), 0)
    oblk = lambda i, be, nu: (i, 0)
    wmap = lambda i, be, nu: (layer, be[jnp.minimum(i, nu[0] - 1)], 0, 0)
    return pl.pallas_call(
        _expert_kernel,
        grid_spec=pltpu.PrefetchScalarGridSpec(
            num_scalar_prefetch=2,
            grid=(n_blk,),
            in_specs=[pl.BlockSpec((MOE_BLK * ROW_TILE, LANES), blk),
                      pl.BlockSpec((1, 1, D_MODEL, D_EXPERT), wmap),
                      pl.BlockSpec((1, 1, D_MODEL, D_EXPERT), wmap),
                      pl.BlockSpec((1, 1, D_EXPERT, D_MODEL), wmap)],
            out_specs=pl.BlockSpec((MOE_BLK * ROW_TILE, LANES), oblk),
            scratch_shapes=[pltpu.VMEM((D_MODEL, D_EXPERT), BF16), pltpu.VMEM((D_MODEL, D_EXPERT), BF16),
                            pltpu.VMEM((D_EXPERT, D_MODEL), BF16)],
        ),
        out_shape=jax.ShapeDtypeStruct(xbuf.shape, F32),
        compiler_params=_cparams(("arbitrary",)),
        name="moe_experts",
    )(blk_exp, n_used, xbuf, wg, wu, wd)


def _combine_kernel(d0_ref, d1_ref, n0_ref, n1_ref, ybuf_ref, x_ref, mod_ref, route_ref, sel_ref, lng_ref, lnb_ref,
                    *rest, alpha, split_blk):
    o_refs, (y_ref, sem) = rest[:-2], rest[-2:]
    i = pl.program_id(0)
    slot = lax.rem(i, 2)

    def gather(dk_refs, s):
        def body(t, carry):
            row = pl.ds(pl.multiple_of(t * ROW_TILE, ROW_TILE), ROW_TILE)
            for k, d_ref in enumerate(dk_refs):
                src = ybuf_ref.at[pl.ds(pl.multiple_of(d_ref[0, t], ROW_TILE), ROW_TILE)]
                pltpu.make_async_copy(src, y_ref.at[s, k, row], sem.at[s, k]).start(priority=k)
            return carry

        lax.fori_loop(0, TM, body, 0, unroll=8)

    @pl.when(i == 0)
    def _():
        gather((d0_ref, d1_ref), slot)

    @pl.when(i + 1 < pl.num_programs(0))
    def _():
        gather((n0_ref, n1_ref), 1 - slot)

    gates = _dot_tn(route_ref[...].astype(BF16), sel_ref[...])
    for k in range(2):
        pltpu.make_async_copy(ybuf_ref.at[pl.ds(0, TM * ROW_TILE)], y_ref.at[slot, k], sem.at[slot, k]).wait()
    m = mod_ref[0]
    g0 = jnp.concatenate([gates[:, 0:LANES]] * ROW_TILE, axis=1)
    g1 = jnp.concatenate([gates[:, LANES:]] * ROW_TILE, axis=1)
    y = _tiles_to_rows(y_ref.at[slot, 0], TM) * g0 + _tiles_to_rows(y_ref.at[slot, 1], TM) * g1
    out = _layer_norm(alpha * x_ref[...] + (1.0 + m[5:6]) * y, lng_ref[...], lnb_ref[...])
    if split_blk is None:
        o_refs[0][...] = out
    else:
        @pl.when(pl.program_id(0) < split_blk)
        def _():
            o_refs[0][...] = out

        @pl.when(pl.program_id(0) >= split_blk)
        def _():
            o_refs[1][...] = out


def _gate_select():
    sel = np.zeros((ROUTE_ROWS, 2 * LANES), np.float32)
    sel[4:7, 0:LANES] = 1.0
    sel[7:10, LANES:] = 1.0
    return jnp.asarray(sel, BF16)


def _combine(ybuf, dest, route, x, mod_l, ln_g, ln_b, segs, *, alpha, split=False):
    nt, d = x.shape
    const = lambda i: (0, 0)
    last = nt // TM - 1
    smem_row = pl.BlockSpec((1, TM), lambda i: (0, i), memory_space=pltpu.SMEM)
    smem_next = pl.BlockSpec((1, TM), lambda i: (0, jnp.minimum(i + 1, last)), memory_space=pltpu.SMEM)
    if split:
        n0 = segs[0][0] * segs[0][1]
        split_blk = n0 // TM
        out_specs = [pl.BlockSpec((TM, d), lambda i: (jnp.minimum(i, split_blk - 1), 0)),
                     pl.BlockSpec((TM, d), lambda i: (jnp.maximum(i - split_blk, 0), 0))]
        out_shape = [jax.ShapeDtypeStruct((n0, d), F32), jax.ShapeDtypeStruct((nt - n0, d), F32)]
    else:
        split_blk = None
        out_specs = [pl.BlockSpec((TM, d), lambda i: (i, 0))]
        out_shape = [jax.ShapeDtypeStruct((nt, d), F32)]
    res = pl.pallas_call(
        functools.partial(_combine_kernel, alpha=alpha, split_blk=split_blk),
        grid=(nt // TM,),
        in_specs=[smem_row, smem_row, smem_next, smem_next,
                  pl.BlockSpec(memory_space=pl.ANY),
                  pl.BlockSpec((TM, d), lambda i: (i, 0)),
                  pl.BlockSpec((1, 6, d), lambda i: (_mod_row(i, TM, segs), 0, 0)),
                  pl.BlockSpec((ROUTE_ROWS, TM), lambda i: (0, i)),
                  pl.BlockSpec((ROUTE_ROWS, 2 * LANES), const),
                  pl.BlockSpec((1, d), const), pl.BlockSpec((1, d), const)],
        out_specs=out_specs,
        out_shape=out_shape,
        scratch_shapes=[pltpu.VMEM((2, 2, TM * ROW_TILE, LANES), F32), pltpu.SemaphoreType.DMA((2, 2))],
        compiler_params=_cparams(("arbitrary",)),
        name="moe_combine",
    )(dest[0], dest[1], dest[0], dest[1], ybuf, x, mod_l, route, _gate_select(), ln_g.reshape(1, d),
      ln_b.reshape(1, d))
    return res if split else res[0]


def _moe_slots(nt):
    return 2 * nt + N_EXPERTS * MOE_BLK


def _moe(x, mod_l, route, counts, ln_g, ln_b, wg, wu, wd, layer, xbuf, segs, *, alpha, split=False):
    n_blk = _moe_slots(x.shape[0]) // MOE_BLK
    cnt = counts[:, 0].astype(jnp.int32)
    padded = (cnt + MOE_BLK - 1) // MOE_BLK * MOE_BLK
    pad_ends = jnp.cumsum(padded)
    pad_starts = (pad_ends - padded).astype(jnp.int32)
    blk_start = jnp.arange(n_blk, dtype=jnp.int32) * MOE_BLK
    blk_exp = jnp.minimum(jnp.sum(pad_ends[None, :] <= blk_start[:, None], axis=1), N_EXPERTS - 1).astype(jnp.int32)
    n_used = (pad_ends[-1:] // MOE_BLK).astype(jnp.int32)
    dest = _dest(route, pad_starts)
    xbuf = _dispatch(x, mod_l, dest, xbuf, segs)
    ybuf = _experts(xbuf, blk_exp, n_used, wg, wu, wd, layer)
    return _combine(ybuf, dest, route, x, mod_l, ln_g, ln_b, segs, alpha=alpha, split=split), xbuf


def _attn_w_in_layout(w_in):
    scale = HEAD_DIM ** -0.5
    qa = w_in[:, 0:512] * scale
    ka = w_in[:, 512:640]
    va = w_in[:, 640:768]
    qb = w_in[:, 768:1280] * scale
    kb = w_in[:, 1280:1792]
    vb = w_in[:, 1792:2304]
    dup = lambda a: jnp.concatenate([a[:, 0:64], a[:, 0:64], a[:, 64:128], a[:, 64:128]], axis=1)
    return jnp.concatenate([qa, dup(ka), dup(va), qb, kb, vb], axis=1).astype(BF16)


def _rec_w_in_layout(w_in):
    d = D_MODEL
    return jnp.concatenate([w_in[:, d:3 * d], w_in[:, 0:d], w_in[:, 3 * d:5 * d]], axis=1).astype(BF16)


def _rope_tables(t_max):
    inv = jnp.power(ROPE_THETA, -jnp.arange(0, HEAD_DIM, 2, dtype=F32) / HEAD_DIM)
    ang = jnp.arange(t_max, dtype=F32)[:, None] * inv[None, :]
    cos = jnp.cos(ang)
    sin = jnp.sin(ang)
    cos128 = jnp.concatenate([cos, cos, cos, cos], axis=1)
    sin128 = jnp.concatenate([-sin, sin, -sin, sin], axis=1)
    return cos128, sin128


def kernel(x_prompt, x_sample, c_prompt, c_sample, ada_w, ada_b, ln_g, ln_b, attn_w_in, attn_sink, nat_rel_bias, attn_w_out, rec_w_in, rec_lb, rec_gnorm, rec_w_out, router_w_group, router_b_group, router_w_expert, router_b_expert, expert_w_gate, expert_w_up, expert_w_down):
    depth = ada_w.shape[0]
    d = x_prompt.shape[-1]
    segs = ((x_prompt.shape[0], x_prompt.shape[1]), (x_sample.shape[0], x_sample.shape[1]))
    alpha = (2 * depth) ** 0.25
    xs = (x_prompt.reshape(-1, d), x_sample.reshape(-1, d))
    nt = xs[0].shape[0] + xs[1].shape[0]
    c = jnp.concatenate([c_prompt, c_sample], axis=0)
    nseq = c.shape[0]
    mod = _adaln(c, ada_w, ada_b).reshape(depth, nseq, 6, d)
    rope_tables = _rope_tables(max(t for _, t in segs))
    xbuf = jnp.zeros((_moe_slots(nt) * ROW_TILE, LANES), F32)
    for l in range(depth):
        mod_l = mod[l]
        i = l // 2
        router = _router_tables(router_w_group[l], router_b_group[l], router_w_expert[l], router_b_expert[l])
        if l % 2 == 0:
            proj = _inproj_attn(xs, mod_l, _attn_w_in_layout(attn_w_in[i]), segs, rope_tables)
            oa = _win_attn(proj, attn_sink[i].astype(F32), segs)
            ob = _nat_attn(proj, _nat_bias_table(nat_rel_bias[i]), segs)
            x, route, counts = _mix_out((oa, ob), attn_w_out[i].astype(BF16), xs, mod_l, ln_g[l, 0], ln_b[l, 0],
                                        router, segs, alpha=alpha)
        else:
            proj_z, proj_qvg = _inproj_rec(xs[0], mod_l, _rec_w_in_layout(rec_w_in[i]), segs)
            o_f, o_b = _hgrn(proj_z, proj_qvg, rec_lb.astype(F32), i, segs)
            x, route, counts = _mix_out((o_f, o_b), rec_w_out[i].astype(BF16), xs, mod_l, ln_g[l, 0], ln_b[l, 0],
                                        router, segs, alpha=alpha, rec_extra=(proj_qvg, rec_gnorm[i].astype(F32)))
        x, xbuf = _moe(x, mod_l, route, counts, ln_g[l, 1], ln_b[l, 1], expert_w_gate, expert_w_up, expert_w_down,
                       l, xbuf, segs, alpha=alpha, split=(l == depth - 1))
        xs = (x,) if l < depth - 1 else x
    return (xs[0].reshape(x_prompt.shape), xs[1].reshape(x_sample.shape))
```

```python
import functools

import numpy as np
import jax
import jax.numpy as jnp
from jax import lax
from jax.experimental import pallas as pl
from jax.experimental.pallas import tpu as pltpu

F32 = jnp.float32
BF16 = jnp.bfloat16
NEG_INF = float("-inf")

D_MODEL = 1024
HEAD_DIM = 64
A_HEADS = 8
A_KV_HEADS = 2
A_WINDOW = 128
A_BLOCK = 128
B_HEADS = 8
GRID_W = 64
NB_ROWS = 8
NB_COLS = 16
ROPE_THETA = 10000.0
C_HEADS = 8
C_KEY_DIM = 128
C_CHUNK = 64
N_GROUPS = 4
EXPERTS_PER_GROUP = 8
N_EXPERTS = 32
D_EXPERT = 512
LN_EPS = 1e-5
RMS_EPS = 1e-6
LOG2E = 1.4426950408889634

LANES = 128
ATTN_W = 2560
ROPE_W = 768
REC_W = 5 * D_MODEL
ROUTER_ROWS = 40
ROUTE_ROWS = 16
ROW_TILE = D_MODEL // LANES

TM = 512
TM_IN = 1024
MOE_BLK = 512
NAT_BLK = 512
REC_BLK = 1024
VMEM_LIMIT = 48 * 1024 * 1024


def _cparams(sem):
    return pltpu.CompilerParams(dimension_semantics=sem, vmem_limit_bytes=VMEM_LIMIT)


def _decode(i, blk, segs):
    out = None
    blk0 = 0
    for (b, t) in segs:
        nb = t // blk
        j = (i - blk0) % nb
        cand = (i - j, j, nb)
        out = cand if out is None else tuple(jnp.where(i < blk0, o, c) for o, c in zip(out, cand))
        blk0 += b * nb
    return out


def _mod_row(i, blk, segs):
    out = None
    blk0 = 0
    row0 = 0
    for (b, t) in segs:
        nb = t // blk
        cand = row0 + (i - blk0) // nb
        out = cand if out is None else jnp.where(i < blk0, out, cand)
        blk0 += b * nb
        row0 += b
    return out


def _pos_block(i, blk, segs):
    return _decode(i, blk, segs)[1]


def _dot(a, b):
    return jnp.dot(a, b, preferred_element_type=F32)


def _dot_nt(a, b):
    return lax.dot_general(a, b, (((1,), (1,)), ((), ())), preferred_element_type=F32)


def _dot_tn(a, b):
    return lax.dot_general(a, b, (((0,), (0,)), ((), ())), preferred_element_type=F32)


def _split2(x):
    hi = x.astype(BF16)
    lo = (x - hi.astype(F32)).astype(BF16)
    return hi, lo


def _layer_norm(x, g, b):
    mu = jnp.mean(x, axis=-1, keepdims=True)
    xc = x - mu
    var = jnp.mean(xc * xc, axis=-1, keepdims=True)
    return xc * lax.rsqrt(var + LN_EPS) * g + b


def _adaln_kernel(c_ref, w_ref, b_ref, o_ref):
    c = c_ref[...]
    cs = c * jax.nn.sigmoid(c)
    c_hi, c_lo = _split2(cs)
    w_hi, w_lo = _split2(w_ref[0])
    o_ref[0] = _dot(c_hi, w_hi) + _dot(c_hi, w_lo) + _dot(c_lo, w_hi) + b_ref[0]


def _adaln(c, ada_w, ada_b):
    depth, d, n = ada_w.shape
    nb = c.shape[0]
    tn = 1536
    return pl.pallas_call(
        _adaln_kernel,
        grid=(depth, n // tn),
        in_specs=[
            pl.BlockSpec((nb, d), lambda l, j: (0, 0)),
            pl.BlockSpec((1, d, tn), lambda l, j: (l, 0, j)),
            pl.BlockSpec((1, 1, tn), lambda l, j: (l, 0, j)),
        ],
        out_specs=pl.BlockSpec((1, nb, tn), lambda l, j: (l, 0, j)),
        out_shape=jax.ShapeDtypeStruct((depth, nb, n), F32),
        compiler_params=_cparams(("parallel", "parallel")),
        name="adaln",
    )(c, ada_w, ada_b.reshape(depth, 1, n))


def _rot_half_pairs(x):
    lane = lax.broadcasted_iota(jnp.int32, x.shape, 1)
    first = (lane & 63) < 32
    return jnp.where(first, pltpu.roll(x, 96, 1), pltpu.roll(x, 32, 1))


def _x_specs(xs, tm):
    d = xs[0].shape[1]
    if len(xs) == 1:
        return [pl.BlockSpec((tm, d), lambda i, *_: (i, 0))]
    nb0 = xs[0].shape[0] // tm
    return [pl.BlockSpec((tm, d), lambda i, *_: (jnp.minimum(i, nb0 - 1), 0)),
            pl.BlockSpec((tm, d), lambda i, *_: (jnp.maximum(i - nb0, 0), 0))]


def _x_block(x_refs, nb0):
    if len(x_refs) == 1:
        return x_refs[0][...]
    return jnp.where(pl.program_id(0) < nb0, x_refs[0][...], x_refs[1][...])


def _modulate_once(x_refs, nb0, mod_ref, xb_ref):
    @pl.when(pl.program_id(1) == 0)
    def _():
        m = mod_ref[0]
        xb_ref[...] = (_x_block(x_refs, nb0) * (1.0 + m[1:2]) + m[0:1]).astype(BF16)


def _inproj_attn_kernel(*refs, nb0):
    x_refs, (mod_ref, w_ref, cos_ref, sin_ref, o_ref, xb_ref) = refs[:-6], refs[-6:]
    _modulate_once(x_refs, nb0, mod_ref, xb_ref)
    y = _dot(xb_ref[...], w_ref[...])

    @pl.when(pl.program_id(1) == 0)
    def _():
        cos = cos_ref[...]
        sin = sin_ref[...]
        for g in range(ROPE_W // LANES):
            blk = y[:, g * LANES:(g + 1) * LANES]
            o_ref[:, g * LANES:(g + 1) * LANES] = (blk * cos + _rot_half_pairs(blk) * sin).astype(o_ref.dtype)
        o_ref[:, ROPE_W:] = y[:, ROPE_W:].astype(o_ref.dtype)

    @pl.when(pl.program_id(1) != 0)
    def _():
        o_ref[...] = y.astype(o_ref.dtype)


def _inproj_rec_kernel(x_ref, mod_ref, w_ref, oz_ref, oq_ref, xb_ref, *, nz):
    _modulate_once((x_ref,), None, mod_ref, xb_ref)
    y = _dot(xb_ref[...], w_ref[...])

    @pl.when(pl.program_id(1) < nz)
    def _():
        oz_ref[...] = y

    @pl.when(pl.program_id(1) >= nz)
    def _():
        oq_ref[...] = y.astype(oq_ref.dtype)


def _inproj_specs(xs, segs, tn):
    d = xs[0].shape[1]
    return _x_specs(xs, TM_IN) + [pl.BlockSpec((1, 6, d), lambda i, j: (_mod_row(i, TM_IN, segs), 0, 0)),
                                  pl.BlockSpec((d, tn), lambda i, j: (0, j))]


def _inproj_attn(xs, mod_l, w, segs, rope_tables):
    nt = sum(x.shape[0] for x in xs)
    d = xs[0].shape[1]
    n = w.shape[1]
    tn = n // 2
    tab = pl.BlockSpec((TM_IN, LANES), lambda i, j: (_pos_block(i, TM_IN, segs), 0))
    return pl.pallas_call(
        functools.partial(_inproj_attn_kernel, nb0=xs[0].shape[0] // TM_IN),
        grid=(nt // TM_IN, n // tn),
        in_specs=_inproj_specs(xs, segs, tn) + [tab, tab],
        out_specs=pl.BlockSpec((TM_IN, tn), lambda i, j: (i, j)),
        out_shape=jax.ShapeDtypeStruct((nt, n), BF16),
        scratch_shapes=[pltpu.VMEM((TM_IN, d), BF16)],
        compiler_params=_cparams(("parallel", "arbitrary")),
        name="inproj_rope",
    )(*xs, mod_l, w, *rope_tables)


def _inproj_rec(x, mod_l, w, segs):
    nt, d = x.shape
    tn = D_MODEL
    nz = 2
    nq = w.shape[1] // tn - nz
    return pl.pallas_call(
        functools.partial(_inproj_rec_kernel, nz=nz),
        grid=(nt // TM_IN, nz + nq),
        in_specs=_inproj_specs((x,), segs, tn),
        out_specs=[pl.BlockSpec((TM_IN, tn), lambda i, j: (i, jnp.minimum(j, nz - 1))),
                   pl.BlockSpec((TM_IN, tn), lambda i, j: (i, jnp.maximum(j - nz, 0)))],
        out_shape=[jax.ShapeDtypeStruct((nt, nz * tn), F32), jax.ShapeDtypeStruct((nt, nq * tn), BF16)],
        scratch_shapes=[pltpu.VMEM((TM_IN, d), BF16)],
        compiler_params=_cparams(("parallel", "arbitrary")),
        name="inproj_rec",
    )(x, mod_l, w)


def _win_attn_kernel(sink_ref, q_ref, k0_ref, k1_ref, k2_ref, v0_ref, v1_ref, v2_ref, o_ref, *, segs):
    i = pl.program_id(0)
    _, j, nb = _decode(i, A_BLOCK, segs)
    ws = jnp.clip(j - 1, 0, nb - 3)
    span = 3 * A_BLOCK
    qpos = j * A_BLOCK + lax.broadcasted_iota(jnp.int32, (A_BLOCK, span), 0)
    kpos = ws * A_BLOCK + lax.broadcasted_iota(jnp.int32, (A_BLOCK, span), 1)
    valid = jnp.abs(qpos - kpos) <= A_WINDOW
    lo = lax.broadcasted_iota(jnp.int32, (A_BLOCK, LANES), 1) < HEAD_DIM
    kfull = jnp.concatenate([k0_ref[...], k1_ref[...], k2_ref[...]], axis=0)
    vfull = jnp.concatenate([v0_ref[...], v1_ref[...], v2_ref[...]], axis=0)
    zero = jnp.zeros((A_BLOCK, LANES), BF16)
    grp = A_HEADS // A_KV_HEADS
    for kvh in range(A_KV_HEADS):
        kk = kfull[:, kvh * LANES:(kvh + 1) * LANES]
        vv = vfull[:, kvh * LANES:(kvh + 1) * LANES]
        parts = []
        for p in range(grp // 2):
            c0 = (kvh * (grp // 2) + p) * LANES
            qp = q_ref[:, c0:c0 + LANES]
            parts.append(jnp.where(lo, qp, zero))
            parts.append(jnp.where(lo, zero, qp))
        qs = jnp.concatenate(parts, axis=0)
        s = _dot_nt(qs, kk)
        probs = []
        inv = []
        for h in range(grp):
            sh = jnp.where(valid, s[h * A_BLOCK:(h + 1) * A_BLOCK], NEG_INF)
            sink = sink_ref[kvh * grp + h]
            m = jnp.maximum(jnp.max(sh, axis=-1, keepdims=True), sink)
            e = jnp.exp(sh - m)
            den = jnp.sum(e, axis=-1, keepdims=True) + jnp.exp(sink - m)
            probs.append(e.astype(BF16))
            inv.append(1.0 / den)
        o = _dot(jnp.concatenate(probs, axis=0), vv)
        for p in range(grp // 2):
            oe = o[(2 * p) * A_BLOCK:(2 * p + 1) * A_BLOCK] * inv[2 * p]
            oo = o[(2 * p + 1) * A_BLOCK:(2 * p + 2) * A_BLOCK] * inv[2 * p + 1]
            c0 = (kvh * (grp // 2) + p) * LANES
            o_ref[:, c0:c0 + LANES] = jnp.where(lo, oe, oo).astype(o_ref.dtype)


def _win_attn(proj, sink, segs):
    nt = proj.shape[0]
    kvw = 2 * LANES

    def kv_map(d, col):
        def f(i):
            base, j, nb = _decode(i, A_BLOCK, segs)
            return (base + jnp.clip(j - 1, 0, nb - 3) + d, col)
        return f

    in_specs = [pl.BlockSpec(memory_space=pltpu.SMEM),
                pl.BlockSpec((A_BLOCK, 512), lambda i: (i, 0))]
    in_specs += [pl.BlockSpec((A_BLOCK, kvw), kv_map(d, 2)) for d in range(3)]
    in_specs += [pl.BlockSpec((A_BLOCK, kvw), kv_map(d, 3)) for d in range(3)]
    return pl.pallas_call(
        functools.partial(_win_attn_kernel, segs=segs),
        grid=(nt // A_BLOCK,),
        in_specs=in_specs,
        out_specs=pl.BlockSpec((A_BLOCK, 512), lambda i: (i, 0)),
        out_shape=jax.ShapeDtypeStruct((nt, 512), BF16),
        compiler_params=_cparams(("parallel",)),
        name="win_attn",
    )(sink, proj, proj, proj, proj, proj, proj, proj)


def _nat_bias_table(rel_bias):
    c = np.arange(GRID_W)
    kc = np.arange(GRID_W)
    qwin = np.clip(c - NB_COLS // 2, 0, GRID_W - NB_COLS)
    ok = (kc[None, :] >= qwin[:, None]) & (kc[None, :] < qwin[:, None] + NB_COLS)
    dc = np.clip(kc[None, :] - c[:, None], -(NB_COLS - 1), NB_COLS - 1) + NB_COLS - 1
    var = np.arange(NB_ROWS)
    irow = np.arange(NB_ROWS)
    dr = irow[None, :] - var[:, None] + NB_ROWS - 1
    t = rel_bias.astype(F32)[:, dr][:, :, :, dc]
    t = jnp.where(jnp.asarray(ok)[None, None, None], t, NEG_INF)
    t = t.transpose(1, 0, 3, 2, 4)
    return t.reshape(NB_ROWS, B_HEADS, GRID_W, NB_ROWS * GRID_W)


def _nat_attn_kernel(q_ref, k0, k1, k2, k3, v0, v1, v2, v3, bias_ref, o_ref, kbuf, vbuf, s_scr, p_scr, *, segs):
    i = pl.program_id(0)
    _, j, nb = _decode(i, NAT_BLK, segs)
    rows = nb * NB_ROWS
    w0 = jnp.clip(NB_ROWS * j - NB_ROWS // 2, 0, rows - 2 * NB_ROWS)
    for d, (kr, vr) in enumerate(((k0, v0), (k1, v1), (k2, v2), (k3, v3))):
        kbuf[d * 256:(d + 1) * 256, :] = kr[...]
        vbuf[d * 256:(d + 1) * 256, :] = vr[...]
    lo = lax.broadcasted_iota(jnp.int32, (GRID_W, LANES), 1) < HEAD_DIM
    zero = jnp.zeros((GRID_W, LANES), BF16)
    nkeys = NB_ROWS * GRID_W

    def window(rr):
        r = NB_ROWS * j + rr
        rs = jnp.clip(r - NB_ROWS // 2, 0, rows - NB_ROWS)
        return r - rs, pl.multiple_of((rs - w0) * GRID_W, GRID_W)

    units = [(rr, hp) for rr in range(NB_ROWS) for hp in range(B_HEADS // 2)]
    for u, (rr, hp) in enumerate(units):
        _, off = window(rr)
        cs = slice(hp * LANES, (hp + 1) * LANES)
        qp = q_ref[rr * GRID_W:(rr + 1) * GRID_W, cs]
        qs = jnp.concatenate([jnp.where(lo, qp, zero), jnp.where(lo, zero, qp)], axis=0)
        s_scr[u] = _dot_nt(qs, kbuf[pl.ds(off, nkeys), cs])
    for u, (rr, hp) in enumerate(units):
        var, _ = window(rr)
        s = s_scr[u] + jnp.concatenate([bias_ref[var, 2 * hp], bias_ref[var, 2 * hp + 1]], axis=0)
        e = jnp.exp(s - jnp.max(s, axis=-1, keepdims=True))
        p_scr[u] = (e * (1.0 / jnp.sum(e, axis=-1, keepdims=True))).astype(BF16)
    for u, (rr, hp) in enumerate(units):
        _, off = window(rr)
        cs = slice(hp * LANES, (hp + 1) * LANES)
        o = _dot(p_scr[u], vbuf[pl.ds(off, nkeys), cs])
        o_ref[rr * GRID_W:(rr + 1) * GRID_W, cs] = jnp.where(lo, o[:GRID_W], o[GRID_W:]).astype(o_ref.dtype)


def _nat_attn(proj, bias_table, segs):
    nt = proj.shape[0]

    def kv_map(d, col):
        def f(i):
            base, j, nb = _decode(i, NAT_BLK, segs)
            return (2 * base + jnp.clip(2 * j - 1, 0, 2 * nb - 4) + d, col)
        return f

    in_specs = [pl.BlockSpec((NAT_BLK, 512), lambda i: (i, 2))]
    in_specs += [pl.BlockSpec((256, 512), kv_map(d, 3)) for d in range(4)]
    in_specs += [pl.BlockSpec((256, 512), kv_map(d, 4)) for d in range(4)]
    in_specs += [pl.BlockSpec(bias_table.shape, lambda i: (0, 0, 0, 0))]
    return pl.pallas_call(
        functools.partial(_nat_attn_kernel, segs=segs),
        grid=(nt // NAT_BLK,),
        in_specs=in_specs,
        out_specs=pl.BlockSpec((NAT_BLK, 512), lambda i: (i, 0)),
        out_shape=jax.ShapeDtypeStruct((nt, 512), BF16),
        scratch_shapes=[pltpu.VMEM((1024, 512), BF16), pltpu.VMEM((1024, 512), BF16),
                        pltpu.VMEM((NB_ROWS * B_HEADS // 2, 2 * GRID_W, NB_ROWS * GRID_W), F32),
                        pltpu.VMEM((NB_ROWS * B_HEADS // 2, 2 * GRID_W, NB_ROWS * GRID_W), BF16)],
        compiler_params=_cparams(("parallel",)),
        name="nat_attn",
    )(*([proj] * 9), bias_table)


def _chunk_cumsum(x, fwd):
    n = x.shape[0]
    x3 = x.reshape(n // 8, 8, LANES)
    sub = lax.broadcasted_iota(jnp.int32, x3.shape, 1)
    s = 1
    while s < 8:
        if fwd:
            x3 = x3 + jnp.where(sub >= s, pltpu.roll(x3, s, 1), 0.0)
        else:
            x3 = x3 + jnp.where(sub < 8 - s, pltpu.roll(x3, 8 - s, 1), 0.0)
        s *= 2
    nv = n // 8
    outs = [None] * nv
    edges = [None] * nv
    run = None
    edge = 7 if fwd else 0
    for j in (range(nv) if fwd else range(nv - 1, -1, -1)):
        cur = x3[j]
        if run is not None:
            cur = cur + run
        outs[j] = cur
        run = jnp.broadcast_to(cur[edge:edge + 1, :], (8, LANES))
        edges[j] = run
    return outs, edges


def _split_rows(outs, edges, m, fwd):
    nv = len(outs)
    if m >= 8:
        g = m // 8
        res = []
        for j in range(nv):
            first = (j // (2 * g)) * 2 * g
            res.append(edges[first + g - 1] if fwd else edges[first + g])
        return res
    sub = lax.broadcasted_iota(jnp.int32, (8, LANES), 0)
    pick = m - 1 if fwd else m
    res = []
    for j in range(nv):
        r = None
        for gi in reversed(range(8 // (2 * m))):
            row = gi * 2 * m + pick
            bc = jnp.broadcast_to(outs[j][row:row + 1, :], (8, LANES))
            r = bc if r is None else jnp.where(sub < (gi + 1) * 2 * m, bc, r)
        res.append(r)
    return res


def _hgrn_chunk(q, z, vb, lower, masks, fwd):
    n = C_CHUNK
    f = lower + (1.0 - lower) * jax.nn.sigmoid(z)
    kk = 1.0 - f
    outs, edges = _chunk_cumsum(jnp.log2(f), fwd)
    b = jnp.concatenate(outs, axis=0)
    total = b[n - 1:n] if fwd else b[0:1]
    rowi = lax.broadcasted_iota(jnp.int32, (n, 1), 0)
    zero = jnp.zeros((n, LANES), BF16)
    xs = []
    m = n // 2
    while m >= 2:
        r = jnp.concatenate(_split_rows(outs, edges, m, fwd), axis=0)
        upper = (rowi & (2 * m - 1)) >= m
        qside = upper if fwd else jnp.logical_not(upper)
        e = jnp.exp2((b - r) * masks[N_PAIR_MASKS + len(xs)])
        xs.append((jnp.where(qside, q, kk) * e).astype(BF16))
        m //= 2
    odd = (rowi & 1) == 1
    qside = odd if fwd else jnp.logical_not(odd)
    xs.append(jnp.where(qside, q * f, kk).astype(BF16))
    ti = lax.broadcasted_iota(jnp.int32, (n, LANES), 0)
    li = lax.broadcasted_iota(jnp.int32, (n, LANES), 1)
    asum = jnp.where(ti == li, jnp.sum(q * kk, axis=-1, keepdims=True), 0.0)
    for p in range(len(xs) // 2):
        x1, x2 = xs[2 * p], xs[2 * p + 1]
        lhs = jnp.concatenate([x1, x2], axis=1)
        rhs = jnp.concatenate([jnp.concatenate([x1, zero], axis=1), jnp.concatenate([zero, x2], axis=1)], axis=0)
        asum = asum + _dot_nt(lhs, rhs) * masks[p]
    ke = (kk * jnp.exp2(total - b)).astype(BF16)
    u = _dot_tn(vb, ke)
    o = _dot(asum.astype(BF16), jnp.concatenate([vb, vb], axis=0))
    qe = (q * jnp.exp2(b)).astype(BF16)
    return o, qe, u, jnp.exp2(total)


N_PAIR_MASKS = 3


def _level_masks(fwd):
    n = C_CHUNK
    t = np.arange(n)[:, None]
    s = np.arange(n)[None, :]
    out = np.zeros((N_PAIR_MASKS + 5, n, 2 * n), np.float32)
    for i, m in enumerate([32, 16, 8, 4, 2, 1]):
        same = (t // (2 * m)) == (s // (2 * m))
        tu = (t % (2 * m)) >= m
        su = (s % (2 * m)) >= m
        ok = same & ((tu & ~su) if fwd else (~tu & su))
        out[i // 2, :, (i % 2) * n:(i % 2 + 1) * n] = ok
        if m >= 2:
            out[N_PAIR_MASKS + i] = np.where(tu if fwd else ~tu, 1.0, -1.0)
    return out


def _hgrn_kernel(lb_ref, mf_ref, mb_ref, qf_ref, zf_ref, vf_ref, qb_ref, zb_ref, vb_ref,
                 of_ref, ob_ref, sf_ref, sb_ref, *, segs, layer):
    i = pl.program_id(1)
    _, j, _ = _decode(i, REC_BLK, segs)

    @pl.when(j == 0)
    def _():
        sf_ref[...] = jnp.zeros_like(sf_ref)
        sb_ref[...] = jnp.zeros_like(sb_ref)

    lb = lb_ref[...]
    ex = jnp.exp(lb - jnp.max(lb, axis=0, keepdims=True))
    p = ex / jnp.sum(ex, axis=0, keepdims=True)
    lower = jnp.sum(p[0:layer + 1], axis=0, keepdims=True) - p[0:1]
    mf, mb = mf_ref, mb_ref
    nch = REC_BLK // C_CHUNK
    dirs = ((qf_ref, zf_ref, vf_ref, of_ref, sf_ref, mf, True), (qb_ref, zb_ref, vb_ref, ob_ref, sb_ref, mb, False))
    parts = ([], [])
    for c in range(nch):
        for d, (q_ref, z_ref, v_ref, _, _, masks, fwd) in enumerate(dirs):
            cc = c if fwd else nch - 1 - c
            sl = slice(cc * C_CHUNK, (cc + 1) * C_CHUNK)
            parts[d].append((sl,) + _hgrn_chunk(q_ref[sl, :].astype(F32), z_ref[sl, :], v_ref[sl, :], lower, masks, fwd))
    for d, (_, _, _, o_ref, s_ref, _, _) in enumerate(dirs):
        st = s_ref[...]
        for (sl, o, qe, u, dec) in parts[d]:
            o_ref[sl, :] = o + _dot_nt(qe, st.astype(BF16))
            st = st * dec + u
        s_ref[...] = st


def _hgrn(proj_z, proj_qvg, rec_lb, layer, segs):
    nt = proj_z.shape[0]
    nrec = rec_lb.shape[0]
    mf = jnp.asarray(_level_masks(True))
    mb = jnp.asarray(_level_masks(False))

    def fwd_map(col):
        return lambda h, i: (i, col * C_HEADS + h)

    def bwd_map(col):
        def f(h, i):
            base, j, nb = _decode(i, REC_BLK, segs)
            return (base + nb - 1 - j, col * C_HEADS + h)
        return f

    blk = (REC_BLK, C_KEY_DIM)
    c3 = lambda h, i: (0, 0, 0)
    in_specs = [pl.BlockSpec((nrec, C_KEY_DIM), lambda h, i: (0, h)),
                pl.BlockSpec(mf.shape, c3), pl.BlockSpec(mb.shape, c3),
                pl.BlockSpec(blk, fwd_map(0)), pl.BlockSpec(blk, fwd_map(0)), pl.BlockSpec(blk, fwd_map(1)),
                pl.BlockSpec(blk, bwd_map(0)), pl.BlockSpec(blk, bwd_map(1)), pl.BlockSpec(blk, bwd_map(1))]
    out_shape = [jax.ShapeDtypeStruct((nt, D_MODEL), F32)] * 2
    return pl.pallas_call(
        functools.partial(_hgrn_kernel, segs=segs, layer=layer),
        grid=(C_HEADS, nt // REC_BLK),
        in_specs=in_specs,
        out_specs=[pl.BlockSpec(blk, fwd_map(0)), pl.BlockSpec(blk, bwd_map(0))],
        out_shape=out_shape,
        scratch_shapes=[pltpu.VMEM((C_KEY_DIM, C_KEY_DIM), F32)] * 2,
        compiler_params=_cparams(("parallel", "arbitrary")),
        name="hgrn",
    )(rec_lb, mf, mb, proj_qvg, proj_z, proj_qvg, proj_qvg, proj_z, proj_qvg)


def _route_rows(logits_t, utri, cnt_ref):
    tm = logits_t.shape[1]
    gi = lax.broadcasted_iota(jnp.int32, (8, tm), 0).astype(F32)
    lg = jnp.where(gi < N_GROUPS, logits_t[0:8], NEG_INF)
    gm = jnp.max(lg, axis=0, keepdims=True)
    p_grp = 1.0 / jnp.sum(jnp.exp(lg - gm), axis=0, keepdims=True)
    grp = jnp.min(jnp.where(lg == gm, gi, 99.0), axis=0, keepdims=True)
    ei = lax.broadcasted_iota(jnp.int32, (N_EXPERTS, tm), 0).astype(F32)
    egrp = (lax.broadcasted_iota(jnp.int32, (N_EXPERTS, tm), 0) >> 3).astype(F32)
    le = jnp.where(egrp == grp, logits_t[8:8 + N_EXPERTS], NEG_INF)
    v1 = jnp.max(le, axis=0, keepdims=True)
    i1 = jnp.min(jnp.where(le == v1, ei, 99.0), axis=0, keepdims=True)
    le2 = jnp.where(ei == i1, NEG_INF, le)
    v2 = jnp.max(le2, axis=0, keepdims=True)
    i2 = jnp.min(jnp.where(le2 == v2, ei, 99.0), axis=0, keepdims=True)
    t = jnp.exp(v2 - v1)
    g1 = p_grp / (1.0 + t)
    g2 = p_grp * t / (1.0 + t)
    o1 = jnp.where(ei == i1, 1.0, 0.0)
    o2 = jnp.where(ei == i2, 1.0, 0.0)
    osum = o1 + o2
    before = _dot(osum.astype(BF16), utri) + cnt_ref[:, 0:1]
    rank1 = jnp.sum(o1 * before, axis=0, keepdims=True)
    rank2 = jnp.sum(o2 * before, axis=0, keepdims=True)
    cnt_ref[...] = cnt_ref[...] + jnp.sum(osum, axis=1, keepdims=True)

    def pieces(g):
        hi = g.astype(BF16).astype(F32)
        r = g - hi
        mid = r.astype(BF16).astype(F32)
        return [hi, mid, r - mid]

    rows = [i1, i2, rank1, rank2] + pieces(g1) + pieces(g2)
    return rows + [jnp.zeros((1, tm), F32)] * (ROUTE_ROWS - len(rows))


def _mix_out_kernel(*refs, rec, alpha, n_x, nb0):
    x_refs, refs = refs[:n_x], refs[n_x:]
    if rec:
        (a_ref, b_ref, g_ref, gn_ref, w_ref, mod_ref, lng_ref, lnb_ref,
         rw_hi_ref, rw_lo_ref, rb_ref, utri_ref, xo_ref, route_ref, cnt_out_ref, cnt_ref) = refs
    else:
        (a_ref, b_ref, w_ref, mod_ref, lng_ref, lnb_ref,
         rw_hi_ref, rw_lo_ref, rb_ref, utri_ref, xo_ref, route_ref, cnt_out_ref, cnt_ref) = refs
    i = pl.program_id(0)

    @pl.when(i == 0)
    def _():
        cnt_ref[...] = jnp.zeros_like(cnt_ref)

    if rec:
        o = a_ref[...] + b_ref[...]
        g = g_ref[...].astype(F32)
        gate = gn_ref[...] * (g * jax.nn.sigmoid(g))
        parts = []
        for h in range(C_HEADS):
            blk = o[:, h * C_KEY_DIM:(h + 1) * C_KEY_DIM]
            ms = jnp.mean(blk * blk, axis=-1, keepdims=True)
            parts.append(blk * lax.rsqrt(ms + RMS_EPS))
        lhs = (jnp.concatenate(parts, axis=1) * gate).astype(BF16)
        y = _dot(lhs, w_ref[...])
    else:
        half = a_ref.shape[1]
        y = _dot(a_ref[...], w_ref[0:half, :]) + _dot(b_ref[...], w_ref[half:, :])
    m = mod_ref[0]
    xn = _layer_norm(alpha * _x_block(x_refs, nb0) + (1.0 + m[2:3]) * y, lng_ref[...], lnb_ref[...])
    xo_ref[...] = xn
    h_hi, h_lo = _split2(xn * (1.0 + m[4:5]) + m[3:4])
    rw_hi = rw_hi_ref[...]
    logits_t = _dot_nt(rw_hi, h_hi) + _dot_nt(rw_hi, h_lo) + _dot_nt(rw_lo_ref[...], h_hi) + rb_ref[:, 0:1]
    for k, row in enumerate(_route_rows(logits_t, utri_ref[...], cnt_ref)):
        route_ref[k:k + 1, :] = row
    cnt_out_ref[...] = cnt_ref[...]


def _router_tables(w_rg, b_rg, w_re, b_re):
    d = w_rg.shape[0]
    wt = jnp.zeros((ROUTER_ROWS, d), F32)
    wt = wt.at[0:N_GROUPS].set(w_rg.astype(F32).T).at[8:8 + N_EXPERTS].set(w_re.astype(F32).T)
    hi = wt.astype(BF16)
    lo = (wt - hi.astype(F32)).astype(BF16)
    rb = jnp.zeros((ROUTER_ROWS,), F32).at[0:N_GROUPS].set(b_rg.astype(F32))
    rb = rb.at[8:8 + N_EXPERTS].set(b_re.astype(F32).reshape(-1))
    return hi, lo, jnp.broadcast_to(rb[:, None], (ROUTER_ROWS, LANES))


def _mix_out(lhs, w_out, xs, mod_l, ln_g, ln_b, router, segs, *, alpha, rec_extra=None):
    nt = sum(x.shape[0] for x in xs)
    d = xs[0].shape[1]
    rec = rec_extra is not None
    rw_hi, rw_lo, rb = router
    idx = np.arange(TM)
    utri = jnp.asarray(idx[:, None] < idx[None, :], BF16)
    row = lambda i: (i, 0)
    const = lambda i: (0, 0)
    a, b = lhs
    in_specs = _x_specs(xs, TM) + [pl.BlockSpec((TM, a.shape[1]), row), pl.BlockSpec((TM, b.shape[1]), row)]
    args = [*xs, a, b]
    if rec:
        proj, gnorm = rec_extra
        in_specs += [pl.BlockSpec((TM, d), lambda i: (i, 2)), pl.BlockSpec((1, d), const)]
        args += [proj, gnorm.reshape(1, d)]
    in_specs += [pl.BlockSpec(w_out.shape, const),
                 pl.BlockSpec((1, 6, d), lambda i: (_mod_row(i, TM, segs), 0, 0)),
                 pl.BlockSpec((1, d), const), pl.BlockSpec((1, d), const),
                 pl.BlockSpec(rw_hi.shape, const), pl.BlockSpec(rw_lo.shape, const),
                 pl.BlockSpec(rb.shape, const), pl.BlockSpec((TM, TM), const)]
    args += [w_out, mod_l, ln_g.reshape(1, d), ln_b.reshape(1, d), rw_hi, rw_lo, rb, utri]
    return pl.pallas_call(
        functools.partial(_mix_out_kernel, rec=rec, alpha=alpha, n_x=len(xs), nb0=xs[0].shape[0] // TM),
        grid=(nt // TM,),
        in_specs=in_specs,
        out_specs=[pl.BlockSpec((TM, d), row), pl.BlockSpec((ROUTE_ROWS, TM), lambda i: (0, i)),
                   pl.BlockSpec((N_EXPERTS, LANES), const)],
        out_shape=[jax.ShapeDtypeStruct((nt, d), F32), jax.ShapeDtypeStruct((ROUTE_ROWS, nt), F32),
                   jax.ShapeDtypeStruct((N_EXPERTS, LANES), F32)],
        scratch_shapes=[pltpu.VMEM((N_EXPERTS, LANES), F32)],
        compiler_params=_cparams(("arbitrary",)),
        name="mix_out_rec" if rec else "mix_out_attn",
    )(*args)


def _rows_to_tiles(ref, x):
    n = x.shape[0]
    for s in range(ROW_TILE):
        ref[pl.ds(s, n, stride=ROW_TILE), :] = x[:, s * LANES:(s + 1) * LANES]


def _tiles_to_rows(ref, n):
    return jnp.concatenate([ref[pl.ds(s, n, stride=ROW_TILE), :] for s in range(ROW_TILE)], axis=1)


def _dest_kernel(ps_ref, route_ref, o0_ref, o1_ref):
    r = route_ref[...]
    for k, o_ref in enumerate((o0_ref, o1_ref)):
        eid = r[k:k + 1].astype(jnp.int32)
        acc = jnp.zeros_like(eid)
        for e in range(N_EXPERTS):
            acc = jnp.where(eid == e, ps_ref[e], acc)
        o_ref[...] = (acc + r[2 + k:3 + k].astype(jnp.int32)) * ROW_TILE


def _dest(route, pad_starts):
    nt = route.shape[1]
    tb = 2048
    return pl.pallas_call(
        _dest_kernel,
        grid_spec=pltpu.PrefetchScalarGridSpec(
            num_scalar_prefetch=1,
            grid=(nt // tb,),
            in_specs=[pl.BlockSpec((ROUTE_ROWS, tb), lambda i, ps: (0, i))],
            out_specs=[pl.BlockSpec((1, tb), lambda i, ps: (0, i))] * 2,
        ),
        out_shape=[jax.ShapeDtypeStruct((1, nt), jnp.int32)] * 2,
        compiler_params=_cparams(("parallel",)),
        name="moe_dest",
    )(pad_starts, route)


def _dispatch_kernel(d0_ref, d1_ref, x_ref, mod_ref, xbuf_in_ref, xbuf_ref, h_ref, sem):
    del xbuf_in_ref
    i = pl.program_id(0)
    slot = lax.rem(i, 2)
    hs = h_ref.at[slot]
    m = mod_ref[0]
    _rows_to_tiles(hs, x_ref[...] * (1.0 + m[4:5]) + m[3:4])

    def body(t, carry):
        src = hs.at[pl.ds(pl.multiple_of(t * ROW_TILE, ROW_TILE), ROW_TILE)]
        for k, d_ref in enumerate((d0_ref, d1_ref)):
            dst = xbuf_ref.at[pl.ds(pl.multiple_of(d_ref[0, t], ROW_TILE), ROW_TILE)]
            pltpu.make_async_copy(src, dst, sem.at[slot, k]).start(priority=k)
        return carry

    lax.fori_loop(0, TM, body, 0, unroll=8)

    def drain(s):
        for k in range(2):
            pltpu.make_async_copy(h_ref.at[s], xbuf_ref.at[pl.ds(0, TM * ROW_TILE)], sem.at[s, k]).wait()

    @pl.when(i > 0)
    def _():
        drain(1 - slot)

    @pl.when(i == pl.num_programs(0) - 1)
    def _():
        drain(slot)


def _dispatch(x, mod_l, dest, xbuf, segs):
    nt, d = x.shape
    smem_row = pl.BlockSpec((1, TM), lambda i: (0, i), memory_space=pltpu.SMEM)
    return pl.pallas_call(
        _dispatch_kernel,
        grid=(nt // TM,),
        in_specs=[smem_row, smem_row,
                  pl.BlockSpec((TM, d), lambda i: (i, 0)),
                  pl.BlockSpec((1, 6, d), lambda i: (_mod_row(i, TM, segs), 0, 0)),
                  pl.BlockSpec(memory_space=pl.ANY)],
        out_specs=pl.BlockSpec(memory_space=pl.ANY),
        out_shape=jax.ShapeDtypeStruct(xbuf.shape, F32),
        scratch_shapes=[pltpu.VMEM((2, TM * ROW_TILE, LANES), F32), pltpu.SemaphoreType.DMA((2, 2))],
        input_output_aliases={4: 0},
        compiler_params=_cparams(("arbitrary",)),
        name="moe_dispatch",
    )(dest[0], dest[1], x, mod_l, xbuf)


def _expert_kernel(be_ref, nu_ref, x_ref, wg_ref, wu_ref, wd_ref, o_ref, wgb_ref, wub_ref, wdb_ref):
    i = pl.program_id(0)
    cur = jnp.minimum(i, nu_ref[0] - 1)
    new_expert = jnp.logical_or(i == 0, be_ref[cur] != be_ref[jnp.maximum(cur - 1, 0)])

    @pl.when(new_expert)
    def _():
        wgb_ref[...] = wg_ref[0, 0].astype(BF16)
        wub_ref[...] = wu_ref[0, 0].astype(BF16)
        wdb_ref[...] = wd_ref[0, 0].astype(BF16)

    @pl.when(i < nu_ref[0])
    def _():
        xb = _tiles_to_rows(x_ref, MOE_BLK).astype(BF16)
        hg = _dot(xb, wgb_ref[...])
        hu = _dot(xb, wub_ref[...])
        hid = (hg * jax.nn.sigmoid(hg) * hu).astype(BF16)
        _rows_to_tiles(o_ref, _dot(hid, wdb_ref[...]))

    @pl.when(i >= nu_ref[0])
    def _():
        o_ref[...] = jnp.zeros_like(o_ref)


def _experts(xbuf, blk_exp, n_used, wg, wu, wd, layer):
    n_blk = xbuf.shape[0] // (MOE_BLK * ROW_TILE)
    blk = lambda i, be, nu: (jnp.minimum(i, nu[0] - 1), 0)
    oblk = lambda i, be, nu: (i, 0)
    wmap = lambda i, be, nu: (layer, be[jnp.minimum(i, nu[0] - 1)], 0, 0)
    return pl.pallas_call(
        _expert_kernel,
        grid_spec=pltpu.PrefetchScalarGridSpec(
            num_scalar_prefetch=2,
            grid=(n_blk,),
            in_specs=[pl.BlockSpec((MOE_BLK * ROW_TILE, LANES), blk),
                      pl.BlockSpec((1, 1, D_MODEL, D_EXPERT), wmap),
                      pl.BlockSpec((1, 1, D_MODEL, D_EXPERT), wmap),
                      pl.BlockSpec((1, 1, D_EXPERT, D_MODEL), wmap)],
            out_specs=pl.BlockSpec((MOE_BLK * ROW_TILE, LANES), oblk),
            scratch_shapes=[pltpu.VMEM((D_MODEL, D_EXPERT), BF16), pltpu.VMEM((D_MODEL, D_EXPERT), BF16),
                            pltpu.VMEM((D_EXPERT, D_MODEL), BF16)],
        ),
        out_shape=jax.ShapeDtypeStruct(xbuf.shape, F32),
        compiler_params=_cparams(("arbitrary",)),
        name="moe_experts",
    )(blk_exp, n_used, xbuf, wg, wu, wd)


def _combine_kernel(d0_ref, d1_ref, n0_ref, n1_ref, ybuf_ref, x_ref, mod_ref, route_ref, sel_ref, lng_ref, lnb_ref,
                    *rest, alpha, split_blk):
    o_refs, (y_ref, sem) = rest[:-2], rest[-2:]
    i = pl.program_id(0)
    slot = lax.rem(i, 2)

    def gather(dk_refs, s):
        def body(t, carry):
            row = pl.ds(pl.multiple_of(t * ROW_TILE, ROW_TILE), ROW_TILE)
            for k, d_ref in enumerate(dk_refs):
                src = ybuf_ref.at[pl.ds(pl.multiple_of(d_ref[0, t], ROW_TILE), ROW_TILE)]
                pltpu.make_async_copy(src, y_ref.at[s, k, row], sem.at[s, k]).start(priority=k)
            return carry

        lax.fori_loop(0, TM, body, 0, unroll=8)

    @pl.when(i == 0)
    def _():
        gather((d0_ref, d1_ref), slot)

    @pl.when(i + 1 < pl.num_programs(0))
    def _():
        gather((n0_ref, n1_ref), 1 - slot)

    gates = _dot_tn(route_ref[...].astype(BF16), sel_ref[...])
    for k in range(2):
        pltpu.make_async_copy(ybuf_ref.at[pl.ds(0, TM * ROW_TILE)], y_ref.at[slot, k], sem.at[slot, k]).wait()
    m = mod_ref[0]
    g0 = jnp.concatenate([gates[:, 0:LANES]] * ROW_TILE, axis=1)
    g1 = jnp.concatenate([gates[:, LANES:]] * ROW_TILE, axis=1)
    y = _tiles_to_rows(y_ref.at[slot, 0], TM) * g0 + _tiles_to_rows(y_ref.at[slot, 1], TM) * g1
    out = _layer_norm(alpha * x_ref[...] + (1.0 + m[5:6]) * y, lng_ref[...], lnb_ref[...])
    if split_blk is None:
        o_refs[0][...] = out
    else:
        @pl.when(pl.program_id(0) < split_blk)
        def _():
            o_refs[0][...] = out

        @pl.when(pl.program_id(0) >= split_blk)
        def _():
            o_refs[1][...] = out


def _gate_select():
    sel = np.zeros((ROUTE_ROWS, 2 * LANES), np.float32)
    sel[4:7, 0:LANES] = 1.0
    sel[7:10, LANES:] = 1.0
    return jnp.asarray(sel, BF16)


def _combine(ybuf, dest, route, x, mod_l, ln_g, ln_b, segs, *, alpha, split=False):
    nt, d = x.shape
    const = lambda i: (0, 0)
    last = nt // TM - 1
    smem_row = pl.BlockSpec((1, TM), lambda i: (0, i), memory_space=pltpu.SMEM)
    smem_next = pl.BlockSpec((1, TM), lambda i: (0, jnp.minimum(i + 1, last)), memory_space=pltpu.SMEM)
    if split:
        n0 = segs[0][0] * segs[0][1]
        split_blk = n0 // TM
        out_specs = [pl.BlockSpec((TM, d), lambda i: (jnp.minimum(i, split_blk - 1), 0)),
                     pl.BlockSpec((TM, d), lambda i: (jnp.maximum(i - split_blk, 0), 0))]
        out_shape = [jax.ShapeDtypeStruct((n0, d), F32), jax.ShapeDtypeStruct((nt - n0, d), F32)]
    else:
        split_blk = None
        out_specs = [pl.BlockSpec((TM, d), lambda i: (i, 0))]
        out_shape = [jax.ShapeDtypeStruct((nt, d), F32)]
    res = pl.pallas_call(
        functools.partial(_combine_kernel, alpha=alpha, split_blk=split_blk),
        grid=(nt // TM,),
        in_specs=[smem_row, smem_row, smem_next, smem_next,
                  pl.BlockSpec(memory_space=pl.ANY),
                  pl.BlockSpec((TM, d), lambda i: (i, 0)),
                  pl.BlockSpec((1, 6, d), lambda i: (_mod_row(i, TM, segs), 0, 0)),
                  pl.BlockSpec((ROUTE_ROWS, TM), lambda i: (0, i)),
                  pl.BlockSpec((ROUTE_ROWS, 2 * LANES), const),
                  pl.BlockSpec((1, d), const), pl.BlockSpec((1, d), const)],
        out_specs=out_specs,
        out_shape=out_shape,
        scratch_shapes=[pltpu.VMEM((2, 2, TM * ROW_TILE, LANES), F32), pltpu.SemaphoreType.DMA((2, 2))],
        compiler_params=_cparams(("arbitrary",)),
        name="moe_combine",
    )(dest[0], dest[1], dest[0], dest[1], ybuf, x, mod_l, route, _gate_select(), ln_g.reshape(1, d),
      ln_b.reshape(1, d))
    return res if split else res[0]


def _moe_slots(nt):
    return 2 * nt + N_EXPERTS * MOE_BLK


def _moe(x, mod_l, route, counts, ln_g, ln_b, wg, wu, wd, layer, xbuf, segs, *, alpha, split=False):
    n_blk = _moe_slots(x.shape[0]) // MOE_BLK
    cnt = counts[:, 0].astype(jnp.int32)
    padded = (cnt + MOE_BLK - 1) // MOE_BLK * MOE_BLK
    pad_ends = jnp.cumsum(padded)
    pad_starts = (pad_ends - padded).astype(jnp.int32)
    blk_start = jnp.arange(n_blk, dtype=jnp.int32) * MOE_BLK
    blk_exp = jnp.minimum(jnp.sum(pad_ends[None, :] <= blk_start[:, None], axis=1), N_EXPERTS - 1).astype(jnp.int32)
    n_used = (pad_ends[-1:] // MOE_BLK).astype(jnp.int32)
    dest = _dest(route, pad_starts)
    xbuf = _dispatch(x, mod_l, dest, xbuf, segs)
    ybuf = _experts(xbuf, blk_exp, n_used, wg, wu, wd, layer)
    return _combine(ybuf, dest, route, x, mod_l, ln_g, ln_b, segs, alpha=alpha, split=split), xbuf


def _attn_w_in_layout(w_in):
    scale = HEAD_DIM ** -0.5
    qa = w_in[:, 0:512] * scale
    ka = w_in[:, 512:640]
    va = w_in[:, 640:768]
    qb = w_in[:, 768:1280] * scale
    kb = w_in[:, 1280:1792]
    vb = w_in[:, 1792:2304]
    dup = lambda a: jnp.concatenate([a[:, 0:64], a[:, 0:64], a[:, 64:128], a[:, 64:128]], axis=1)
    return jnp.concatenate([qa, dup(ka), dup(va), qb, kb, vb], axis=1).astype(BF16)


def _rec_w_in_layout(w_in):
    d = D_MODEL
    return jnp.concatenate([w_in[:, d:3 * d], w_in[:, 0:d], w_in[:, 3 * d:5 * d]], axis=1).astype(BF16)


def _rope_tables(t_max):
    inv = jnp.power(ROPE_THETA, -jnp.arange(0, HEAD_DIM, 2, dtype=F32) / HEAD_DIM)
    ang = jnp.arange(t_max, dtype=F32)[:, None] * inv[None, :]
    cos = jnp.cos(ang)
    sin = jnp.sin(ang)
    cos128 = jnp.concatenate([cos, cos, cos, cos], axis=1)
    sin128 = jnp.concatenate([-sin, sin, -sin, sin], axis=1)
    return cos128, sin128


def kernel(x_prompt, x_sample, c_prompt, c_sample, ada_w, ada_b, ln_g, ln_b, attn_w_in, attn_sink, nat_rel_bias, attn_w_out, rec_w_in, rec_lb, rec_gnorm, rec_w_out, router_w_group, router_b_group, router_w_expert, router_b_expert, expert_w_gate, expert_w_up, expert_w_down):
    depth = ada_w.shape[0]
    d = x_prompt.shape[-1]
    segs = ((x_prompt.shape[0], x_prompt.shape[1]), (x_sample.shape[0], x_sample.shape[1]))
    alpha = (2 * depth) ** 0.25
    xs = (x_prompt.reshape(-1, d), x_sample.reshape(-1, d))
    nt = xs[0].shape[0] + xs[1].shape[0]
    c = jnp.concatenate([c_prompt, c_sample], axis=0)
    nseq = c.shape[0]
    mod = _adaln(c, ada_w, ada_b).reshape(depth, nseq, 6, d)
    rope_tables = _rope_tables(max(t for _, t in segs))
    xbuf = jnp.zeros((_moe_slots(nt) * ROW_TILE, LANES), F32)
    for l in range(depth):
        mod_l = mod[l]
        i = l // 2
        router = _router_tables(router_w_group[l], router_b_group[l], router_w_expert[l], router_b_expert[l])
        if l % 2 == 0:
            proj = _inproj_attn(xs, mod_l, _attn_w_in_layout(attn_w_in[i]), segs, rope_tables)
            oa = _win_attn(proj, attn_sink[i].astype(F32), segs)
            ob = _nat_attn(proj, _nat_bias_table(nat_rel_bias[i]), segs)
            x, route, counts = _mix_out((oa, ob), attn_w_out[i].astype(BF16), xs, mod_l, ln_g[l, 0], ln_b[l, 0],
                                        router, segs, alpha=alpha)
        else:
            proj_z, proj_qvg = _inproj_rec(xs[0], mod_l, _rec_w_in_layout(rec_w_in[i]), segs)
            o_f, o_b = _hgrn(proj_z, proj_qvg, rec_lb.astype(F32), i, segs)
            x, route, counts = _mix_out((o_f, o_b), rec_w_out[i].astype(BF16), xs, mod_l, ln_g[l, 0], ln_b[l, 0],
                                        router, segs, alpha=alpha, rec_extra=(proj_qvg, rec_gnorm[i].astype(F32)))
        x, xbuf = _moe(x, mod_l, route, counts, ln_g[l, 1], ln_b[l, 1], expert_w_gate, expert_w_up, expert_w_down,
                       l, xbuf, segs, alpha=alpha, split=(l == depth - 1))
        xs = (x,) if l < depth - 1 else x
    return (xs[0].reshape(x_prompt.shape), xs[1].reshape(x_sample.shape))
```

```python
import functools

import numpy as np
import jax
import jax.numpy as jnp
from jax import lax
from jax.experimental import pallas as pl
from jax.experimental.pallas import tpu as pltpu

F32 = jnp.float32
BF16 = jnp.bfloat16
NEG_INF = float("-inf")

D_MODEL = 1024
HEAD_DIM = 64
A_HEADS = 8
A_KV_HEADS = 2
A_WINDOW = 128
A_BLOCK = 128
B_HEADS = 8
GRID_W = 64
NB_ROWS = 8
NB_COLS = 16
ROPE_THETA = 10000.0
C_HEADS = 8
C_KEY_DIM = 128
C_CHUNK = 64
N_GROUPS = 4
EXPERTS_PER_GROUP = 8
N_EXPERTS = 32
D_EXPERT = 512
LN_EPS = 1e-5
RMS_EPS = 1e-6
LOG2E = 1.4426950408889634

LANES = 128
ATTN_W = 2560
ROPE_W = 768
REC_W = 5 * D_MODEL
ROUTER_ROWS = 40
ROUTE_ROWS = 16
ROW_TILE = D_MODEL // LANES

TM = 512
TM_IN = 1024
MOE_BLK = 512
NAT_BLK = 512
REC_BLK = 2048
VMEM_LIMIT = 48 * 1024 * 1024


def _cparams(sem):
    return pltpu.CompilerParams(dimension_semantics=sem, vmem_limit_bytes=VMEM_LIMIT)


def _decode(i, blk, segs):
    out = None
    blk0 = 0
    for (b, t) in segs:
        nb = t // blk
        j = (i - blk0) % nb
        cand = (i - j, j, nb)
        out = cand if out is None else tuple(jnp.where(i < blk0, o, c) for o, c in zip(out, cand))
        blk0 += b * nb
    return out


def _mod_row(i, blk, segs):
    out = None
    blk0 = 0
    row0 = 0
    for (b, t) in segs:
        nb = t // blk
        cand = row0 + (i - blk0) // nb
        out = cand if out is None else jnp.where(i < blk0, out, cand)
        blk0 += b * nb
        row0 += b
    return out


def _pos_block(i, blk, segs):
    return _decode(i, blk, segs)[1]


def _dot(a, b):
    return jnp.dot(a, b, preferred_element_type=F32)


def _dot_nt(a, b):
    return lax.dot_general(a, b, (((1,), (1,)), ((), ())), preferred_element_type=F32)


def _dot_tn(a, b):
    return lax.dot_general(a, b, (((0,), (0,)), ((), ())), preferred_element_type=F32)


def _split2(x):
    hi = x.astype(BF16)
    lo = (x - hi.astype(F32)).astype(BF16)
    return hi, lo


def _layer_norm(x, g, b):
    mu = jnp.mean(x, axis=-1, keepdims=True)
    xc = x - mu
    var = jnp.mean(xc * xc, axis=-1, keepdims=True)
    return xc * lax.rsqrt(var + LN_EPS) * g + b


def _adaln_kernel(c_ref, w_ref, b_ref, o_ref):
    c = c_ref[...]
    cs = c * jax.nn.sigmoid(c)
    c_hi, c_lo = _split2(cs)
    w_hi, w_lo = _split2(w_ref[0])
    o_ref[0] = _dot(c_hi, w_hi) + _dot(c_hi, w_lo) + _dot(c_lo, w_hi) + b_ref[0]


def _adaln(c, ada_w, ada_b):
    depth, d, n = ada_w.shape
    nb = c.shape[0]
    tn = 1536
    return pl.pallas_call(
        _adaln_kernel,
        grid=(depth, n // tn),
        in_specs=[
            pl.BlockSpec((nb, d), lambda l, j: (0, 0)),
            pl.BlockSpec((1, d, tn), lambda l, j: (l, 0, j)),
            pl.BlockSpec((1, 1, tn), lambda l, j: (l, 0, j)),
        ],
        out_specs=pl.BlockSpec((1, nb, tn), lambda l, j: (l, 0, j)),
        out_shape=jax.ShapeDtypeStruct((depth, nb, n), F32),
        compiler_params=_cparams(("parallel", "parallel")),
        name="adaln",
    )(c, ada_w, ada_b.reshape(depth, 1, n))


def _rot_half_pairs(x):
    lane = lax.broadcasted_iota(jnp.int32, x.shape, 1)
    first = (lane & 63) < 32
    return jnp.where(first, pltpu.roll(x, 96, 1), pltpu.roll(x, 32, 1))


def _x_specs(xs, tm):
    d = xs[0].shape[1]
    if len(xs) == 1:
        return [pl.BlockSpec((tm, d), lambda i, *_: (i, 0))]
    nb0 = xs[0].shape[0] // tm
    return [pl.BlockSpec((tm, d), lambda i, *_: (jnp.minimum(i, nb0 - 1), 0)),
            pl.BlockSpec((tm, d), lambda i, *_: (jnp.maximum(i - nb0, 0), 0))]


def _x_block(x_refs, nb0):
    if len(x_refs) == 1:
        return x_refs[0][...]
    return jnp.where(pl.program_id(0) < nb0, x_refs[0][...], x_refs[1][...])


def _modulate_once(x_refs, nb0, mod_ref, xb_ref):
    @pl.when(pl.program_id(1) == 0)
    def _():
        m = mod_ref[0]
        xb_ref[...] = (_x_block(x_refs, nb0) * (1.0 + m[1:2]) + m[0:1]).astype(BF16)


def _inproj_attn_kernel(*refs, nb0):
    x_refs, (mod_ref, w_ref, cos_ref, sin_ref, o_ref, xb_ref) = refs[:-6], refs[-6:]
    _modulate_once(x_refs, nb0, mod_ref, xb_ref)
    y = _dot(xb_ref[...], w_ref[...])

    @pl.when(pl.program_id(1) == 0)
    def _():
        cos = cos_ref[...]
        sin = sin_ref[...]
        for g in range(ROPE_W // LANES):
            blk = y[:, g * LANES:(g + 1) * LANES]
            o_ref[:, g * LANES:(g + 1) * LANES] = (blk * cos + _rot_half_pairs(blk) * sin).astype(o_ref.dtype)
        o_ref[:, ROPE_W:] = y[:, ROPE_W:].astype(o_ref.dtype)

    @pl.when(pl.program_id(1) != 0)
    def _():
        o_ref[...] = y.astype(o_ref.dtype)


def _inproj_rec_kernel(x_ref, mod_ref, w_ref, oz_ref, oq_ref, xb_ref, *, nz):
    _modulate_once((x_ref,), None, mod_ref, xb_ref)
    y = _dot(xb_ref[...], w_ref[...])

    @pl.when(pl.program_id(1) < nz)
    def _():
        oz_ref[...] = y

    @pl.when(pl.program_id(1) >= nz)
    def _():
        oq_ref[...] = y.astype(oq_ref.dtype)


def _inproj_specs(xs, segs, tn):
    d = xs[0].shape[1]
    return _x_specs(xs, TM_IN) + [pl.BlockSpec((1, 6, d), lambda i, j: (_mod_row(i, TM_IN, segs), 0, 0)),
                                  pl.BlockSpec((d, tn), lambda i, j: (0, j))]


def _inproj_attn(xs, mod_l, w, segs, rope_tables):
    nt = sum(x.shape[0] for x in xs)
    d = xs[0].shape[1]
    n = w.shape[1]
    tn = n // 2
    tab = pl.BlockSpec((TM_IN, LANES), lambda i, j: (_pos_block(i, TM_IN, segs), 0))
    return pl.pallas_call(
        functools.partial(_inproj_attn_kernel, nb0=xs[0].shape[0] // TM_IN),
        grid=(nt // TM_IN, n // tn),
        in_specs=_inproj_specs(xs, segs, tn) + [tab, tab],
        out_specs=pl.BlockSpec((TM_IN, tn), lambda i, j: (i, j)),
        out_shape=jax.ShapeDtypeStruct((nt, n), BF16),
        scratch_shapes=[pltpu.VMEM((TM_IN, d), BF16)],
        compiler_params=_cparams(("parallel", "arbitrary")),
        name="inproj_rope",
    )(*xs, mod_l, w, *rope_tables)


def _inproj_rec(x, mod_l, w, segs):
    nt, d = x.shape
    tn = D_MODEL
    nz = 2
    nq = w.shape[1] // tn - nz
    return pl.pallas_call(
        functools.partial(_inproj_rec_kernel, nz=nz),
        grid=(nt // TM_IN, nz + nq),
        in_specs=_inproj_specs((x,), segs, tn),
        out_specs=[pl.BlockSpec((TM_IN, tn), lambda i, j: (i, jnp.minimum(j, nz - 1))),
                   pl.BlockSpec((TM_IN, tn), lambda i, j: (i, jnp.maximum(j - nz, 0)))],
        out_shape=[jax.ShapeDtypeStruct((nt, nz * tn), F32), jax.ShapeDtypeStruct((nt, nq * tn), BF16)],
        scratch_shapes=[pltpu.VMEM((TM_IN, d), BF16)],
        compiler_params=_cparams(("parallel", "arbitrary")),
        name="inproj_rec",
    )(x, mod_l, w)


def _win_attn_kernel(sink_ref, q_ref, k0_ref, k1_ref, k2_ref, v0_ref, v1_ref, v2_ref, o_ref, *, segs):
    i = pl.program_id(0)
    _, j, nb = _decode(i, A_BLOCK, segs)
    ws = jnp.clip(j - 1, 0, nb - 3)
    span = 3 * A_BLOCK
    qpos = j * A_BLOCK + lax.broadcasted_iota(jnp.int32, (A_BLOCK, span), 0)
    kpos = ws * A_BLOCK + lax.broadcasted_iota(jnp.int32, (A_BLOCK, span), 1)
    valid = jnp.abs(qpos - kpos) <= A_WINDOW
    lo = lax.broadcasted_iota(jnp.int32, (A_BLOCK, LANES), 1) < HEAD_DIM
    kfull = jnp.concatenate([k0_ref[...], k1_ref[...], k2_ref[...]], axis=0)
    vfull = jnp.concatenate([v0_ref[...], v1_ref[...], v2_ref[...]], axis=0)
    zero = jnp.zeros((A_BLOCK, LANES), BF16)
    grp = A_HEADS // A_KV_HEADS
    for kvh in range(A_KV_HEADS):
        kk = kfull[:, kvh * LANES:(kvh + 1) * LANES]
        vv = vfull[:, kvh * LANES:(kvh + 1) * LANES]
        parts = []
        for p in range(grp // 2):
            c0 = (kvh * (grp // 2) + p) * LANES
            qp = q_ref[:, c0:c0 + LANES]
            parts.append(jnp.where(lo, qp, zero))
            parts.append(jnp.where(lo, zero, qp))
        qs = jnp.concatenate(parts, axis=0)
        s = _dot_nt(qs, kk)
        probs = []
        inv = []
        for h in range(grp):
            sh = jnp.where(valid, s[h * A_BLOCK:(h + 1) * A_BLOCK], NEG_INF)
            sink = sink_ref[kvh * grp + h]
            m = jnp.maximum(jnp.max(sh, axis=-1, keepdims=True), sink)
            e = jnp.exp(sh - m)
            den = jnp.sum(e, axis=-1, keepdims=True) + jnp.exp(sink - m)
            probs.append(e.astype(BF16))
            inv.append(1.0 / den)
        o = _dot(jnp.concatenate(probs, axis=0), vv)
        for p in range(grp // 2):
            oe = o[(2 * p) * A_BLOCK:(2 * p + 1) * A_BLOCK] * inv[2 * p]
            oo = o[(2 * p + 1) * A_BLOCK:(2 * p + 2) * A_BLOCK] * inv[2 * p + 1]
            c0 = (kvh * (grp // 2) + p) * LANES
            o_ref[:, c0:c0 + LANES] = jnp.where(lo, oe, oo).astype(o_ref.dtype)


def _win_attn(proj, sink, segs):
    nt = proj.shape[0]
    kvw = 2 * LANES

    def kv_map(d, col):
        def f(i):
            base, j, nb = _decode(i, A_BLOCK, segs)
            return (base + jnp.clip(j - 1, 0, nb - 3) + d, col)
        return f

    in_specs = [pl.BlockSpec(memory_space=pltpu.SMEM),
                pl.BlockSpec((A_BLOCK, 512), lambda i: (i, 0))]
    in_specs += [pl.BlockSpec((A_BLOCK, kvw), kv_map(d, 2)) for d in range(3)]
    in_specs += [pl.BlockSpec((A_BLOCK, kvw), kv_map(d, 3)) for d in range(3)]
    return pl.pallas_call(
        functools.partial(_win_attn_kernel, segs=segs),
        grid=(nt // A_BLOCK,),
        in_specs=in_specs,
        out_specs=pl.BlockSpec((A_BLOCK, 512), lambda i: (i, 0)),
        out_shape=jax.ShapeDtypeStruct((nt, 512), BF16),
        compiler_params=_cparams(("parallel",)),
        name="win_attn",
    )(sink, proj, proj, proj, proj, proj, proj, proj)


def _nat_bias_table(rel_bias):
    c = np.arange(GRID_W)
    kc = np.arange(GRID_W)
    qwin = np.clip(c - NB_COLS // 2, 0, GRID_W - NB_COLS)
    ok = (kc[None, :] >= qwin[:, None]) & (kc[None, :] < qwin[:, None] + NB_COLS)
    dc = np.clip(kc[None, :] - c[:, None], -(NB_COLS - 1), NB_COLS - 1) + NB_COLS - 1
    var = np.arange(NB_ROWS)
    irow = np.arange(NB_ROWS)
    dr = irow[None, :] - var[:, None] + NB_ROWS - 1
    t = rel_bias.astype(F32)[:, dr][:, :, :, dc]
    t = jnp.where(jnp.asarray(ok)[None, None, None], t, NEG_INF)
    t = t.transpose(1, 0, 3, 2, 4)
    return t.reshape(NB_ROWS, B_HEADS, GRID_W, NB_ROWS * GRID_W)


def _nat_attn_kernel(q_ref, k0, k1, k2, k3, v0, v1, v2, v3, bias_ref, o_ref, kbuf, vbuf, s_scr, p_scr, *, segs):
    i = pl.program_id(0)
    _, j, nb = _decode(i, NAT_BLK, segs)
    rows = nb * NB_ROWS
    w0 = jnp.clip(NB_ROWS * j - NB_ROWS // 2, 0, rows - 2 * NB_ROWS)
    for d, (kr, vr) in enumerate(((k0, v0), (k1, v1), (k2, v2), (k3, v3))):
        kbuf[d * 256:(d + 1) * 256, :] = kr[...]
        vbuf[d * 256:(d + 1) * 256, :] = vr[...]
    lo = lax.broadcasted_iota(jnp.int32, (GRID_W, LANES), 1) < HEAD_DIM
    zero = jnp.zeros((GRID_W, LANES), BF16)
    nkeys = NB_ROWS * GRID_W

    def window(rr):
        r = NB_ROWS * j + rr
        rs = jnp.clip(r - NB_ROWS // 2, 0, rows - NB_ROWS)
        return r - rs, pl.multiple_of((rs - w0) * GRID_W, GRID_W)

    units = [(rr, hp) for rr in range(NB_ROWS) for hp in range(B_HEADS // 2)]
    for u, (rr, hp) in enumerate(units):
        _, off = window(rr)
        cs = slice(hp * LANES, (hp + 1) * LANES)
        qp = q_ref[rr * GRID_W:(rr + 1) * GRID_W, cs]
        qs = jnp.concatenate([jnp.where(lo, qp, zero), jnp.where(lo, zero, qp)], axis=0)
        s_scr[u] = _dot_nt(qs, kbuf[pl.ds(off, nkeys), cs])
    for u, (rr, hp) in enumerate(units):
        var, _ = window(rr)
        s = s_scr[u] + jnp.concatenate([bias_ref[var, 2 * hp], bias_ref[var, 2 * hp + 1]], axis=0)
        e = jnp.exp(s - jnp.max(s, axis=-1, keepdims=True))
        p_scr[u] = (e * (1.0 / jnp.sum(e, axis=-1, keepdims=True))).astype(BF16)
    for u, (rr, hp) in enumerate(units):
        _, off = window(rr)
        cs = slice(hp * LANES, (hp + 1) * LANES)
        o = _dot(p_scr[u], vbuf[pl.ds(off, nkeys), cs])
        o_ref[rr * GRID_W:(rr + 1) * GRID_W, cs] = jnp.where(lo, o[:GRID_W], o[GRID_W:]).astype(o_ref.dtype)


def _nat_attn(proj, bias_table, segs):
    nt = proj.shape[0]

    def kv_map(d, col):
        def f(i):
            base, j, nb = _decode(i, NAT_BLK, segs)
            return (2 * base + jnp.clip(2 * j - 1, 0, 2 * nb - 4) + d, col)
        return f

    in_specs = [pl.BlockSpec((NAT_BLK, 512), lambda i: (i, 2))]
    in_specs += [pl.BlockSpec((256, 512), kv_map(d, 3)) for d in range(4)]
    in_specs += [pl.BlockSpec((256, 512), kv_map(d, 4)) for d in range(4)]
    in_specs += [pl.BlockSpec(bias_table.shape, lambda i: (0, 0, 0, 0))]
    return pl.pallas_call(
        functools.partial(_nat_attn_kernel, segs=segs),
        grid=(nt // NAT_BLK,),
        in_specs=in_specs,
        out_specs=pl.BlockSpec((NAT_BLK, 512), lambda i: (i, 0)),
        out_shape=jax.ShapeDtypeStruct((nt, 512), BF16),
        scratch_shapes=[pltpu.VMEM((1024, 512), BF16), pltpu.VMEM((1024, 512), BF16),
                        pltpu.VMEM((NB_ROWS * B_HEADS // 2, 2 * GRID_W, NB_ROWS * GRID_W), F32),
                        pltpu.VMEM((NB_ROWS * B_HEADS // 2, 2 * GRID_W, NB_ROWS * GRID_W), BF16)],
        compiler_params=_cparams(("parallel",)),
        name="nat_attn",
    )(*([proj] * 9), bias_table)


def _chunk_cumsum(x, fwd):
    n = x.shape[0]
    x3 = x.reshape(n // 8, 8, LANES)
    sub = lax.broadcasted_iota(jnp.int32, x3.shape, 1)
    s = 1
    while s < 8:
        if fwd:
            x3 = x3 + jnp.where(sub >= s, pltpu.roll(x3, s, 1), 0.0)
        else:
            x3 = x3 + jnp.where(sub < 8 - s, pltpu.roll(x3, 8 - s, 1), 0.0)
        s *= 2
    nv = n // 8
    outs = [None] * nv
    edges = [None] * nv
    run = None
    edge = 7 if fwd else 0
    for j in (range(nv) if fwd else range(nv - 1, -1, -1)):
        cur = x3[j]
        if run is not None:
            cur = cur + run
        outs[j] = cur
        run = jnp.broadcast_to(cur[edge:edge + 1, :], (8, LANES))
        edges[j] = run
    return outs, edges


def _split_rows(outs, edges, m, fwd):
    nv = len(outs)
    if m >= 8:
        g = m // 8
        res = []
        for j in range(nv):
            first = (j // (2 * g)) * 2 * g
            res.append(edges[first + g - 1] if fwd else edges[first + g])
        return res
    sub = lax.broadcasted_iota(jnp.int32, (8, LANES), 0)
    pick = m - 1 if fwd else m
    res = []
    for j in range(nv):
        r = None
        for gi in reversed(range(8 // (2 * m))):
            row = gi * 2 * m + pick
            bc = jnp.broadcast_to(outs[j][row:row + 1, :], (8, LANES))
            r = bc if r is None else jnp.where(sub < (gi + 1) * 2 * m, bc, r)
        res.append(r)
    return res


def _hgrn_chunk(q, z, vb, lower, masks, fwd):
    n = C_CHUNK
    f = lower + (1.0 - lower) * jax.nn.sigmoid(z)
    kk = 1.0 - f
    outs, edges = _chunk_cumsum(jnp.log2(f), fwd)
    b = jnp.concatenate(outs, axis=0)
    total = b[n - 1:n] if fwd else b[0:1]
    rowi = lax.broadcasted_iota(jnp.int32, (n, 1), 0)
    zero = jnp.zeros((n, LANES), BF16)
    xs = []
    m = n // 2
    while m >= 2:
        r = jnp.concatenate(_split_rows(outs, edges, m, fwd), axis=0)
        upper = (rowi & (2 * m - 1)) >= m
        qside = upper if fwd else jnp.logical_not(upper)
        e = jnp.exp2((b - r) * masks[N_PAIR_MASKS + len(xs)])
        xs.append((jnp.where(qside, q, kk) * e).astype(BF16))
        m //= 2
    odd = (rowi & 1) == 1
    qside = odd if fwd else jnp.logical_not(odd)
    xs.append(jnp.where(qside, q * f, kk).astype(BF16))
    ti = lax.broadcasted_iota(jnp.int32, (n, LANES), 0)
    li = lax.broadcasted_iota(jnp.int32, (n, LANES), 1)
    asum = jnp.where(ti == li, jnp.sum(q * kk, axis=-1, keepdims=True), 0.0)
    for p in range(len(xs) // 2):
        x1, x2 = xs[2 * p], xs[2 * p + 1]
        lhs = jnp.concatenate([x1, x2], axis=1)
        rhs = jnp.concatenate([jnp.concatenate([x1, zero], axis=1), jnp.concatenate([zero, x2], axis=1)], axis=0)
        asum = asum + _dot_nt(lhs, rhs) * masks[p]
    ke = (kk * jnp.exp2(total - b)).astype(BF16)
    u = _dot_tn(vb, ke)
    o = _dot(asum.astype(BF16), jnp.concatenate([vb, vb], axis=0))
    qe = (q * jnp.exp2(b)).astype(BF16)
    return o, qe, u, jnp.exp2(total)


N_PAIR_MASKS = 3


def _level_masks(fwd):
    n = C_CHUNK
    t = np.arange(n)[:, None]
    s = np.arange(n)[None, :]
    out = np.zeros((N_PAIR_MASKS + 5, n, 2 * n), np.float32)
    for i, m in enumerate([32, 16, 8, 4, 2, 1]):
        same = (t // (2 * m)) == (s // (2 * m))
        tu = (t % (2 * m)) >= m
        su = (s % (2 * m)) >= m
        ok = same & ((tu & ~su) if fwd else (~tu & su))
        out[i // 2, :, (i % 2) * n:(i % 2 + 1) * n] = ok
        if m >= 2:
            out[N_PAIR_MASKS + i] = np.where(tu if fwd else ~tu, 1.0, -1.0)
    return out


def _hgrn_kernel(lb_ref, mf_ref, mb_ref, qf_ref, zf_ref, vf_ref, qb_ref, zb_ref, vb_ref,
                 of_ref, ob_ref, sf_ref, sb_ref, *, segs, layer):
    i = pl.program_id(1)
    _, j, _ = _decode(i, REC_BLK, segs)

    @pl.when(j == 0)
    def _():
        sf_ref[...] = jnp.zeros_like(sf_ref)
        sb_ref[...] = jnp.zeros_like(sb_ref)

    lb = lb_ref[...]
    ex = jnp.exp(lb - jnp.max(lb, axis=0, keepdims=True))
    p = ex / jnp.sum(ex, axis=0, keepdims=True)
    lower = jnp.sum(p[0:layer + 1], axis=0, keepdims=True) - p[0:1]
    mf, mb = mf_ref, mb_ref
    nch = REC_BLK // C_CHUNK
    dirs = ((qf_ref, zf_ref, vf_ref, of_ref, sf_ref, mf, True), (qb_ref, zb_ref, vb_ref, ob_ref, sb_ref, mb, False))
    parts = ([], [])
    for c in range(nch):
        for d, (q_ref, z_ref, v_ref, _, _, masks, fwd) in enumerate(dirs):
            cc = c if fwd else nch - 1 - c
            sl = slice(cc * C_CHUNK, (cc + 1) * C_CHUNK)
            parts[d].append((sl,) + _hgrn_chunk(q_ref[sl, :].astype(F32), z_ref[sl, :], v_ref[sl, :], lower, masks, fwd))
    for d, (_, _, _, o_ref, s_ref, _, _) in enumerate(dirs):
        st = s_ref[...]
        for (sl, o, qe, u, dec) in parts[d]:
            o_ref[sl, :] = o + _dot_nt(qe, st.astype(BF16))
            st = st * dec + u
        s_ref[...] = st


def _hgrn(proj_z, proj_qvg, rec_lb, layer, segs):
    nt = proj_z.shape[0]
    nrec = rec_lb.shape[0]
    mf = jnp.asarray(_level_masks(True))
    mb = jnp.asarray(_level_masks(False))

    def fwd_map(col):
        return lambda h, i: (i, col * C_HEADS + h)

    def bwd_map(col):
        def f(h, i):
            base, j, nb = _decode(i, REC_BLK, segs)
            return (base + nb - 1 - j, col * C_HEADS + h)
        return f

    blk = (REC_BLK, C_KEY_DIM)
    c3 = lambda h, i: (0, 0, 0)
    in_specs = [pl.BlockSpec((nrec, C_KEY_DIM), lambda h, i: (0, h)),
                pl.BlockSpec(mf.shape, c3), pl.BlockSpec(mb.shape, c3),
                pl.BlockSpec(blk, fwd_map(0)), pl.BlockSpec(blk, fwd_map(0)), pl.BlockSpec(blk, fwd_map(1)),
                pl.BlockSpec(blk, bwd_map(0)), pl.BlockSpec(blk, bwd_map(1)), pl.BlockSpec(blk, bwd_map(1))]
    out_shape = [jax.ShapeDtypeStruct((nt, D_MODEL), F32)] * 2
    return pl.pallas_call(
        functools.partial(_hgrn_kernel, segs=segs, layer=layer),
        grid=(C_HEADS, nt // REC_BLK),
        in_specs=in_specs,
        out_specs=[pl.BlockSpec(blk, fwd_map(0)), pl.BlockSpec(blk, bwd_map(0))],
        out_shape=out_shape,
        scratch_shapes=[pltpu.VMEM((C_KEY_DIM, C_KEY_DIM), F32)] * 2,
        compiler_params=_cparams(("parallel", "arbitrary")),
        name="hgrn",
    )(rec_lb, mf, mb, proj_qvg, proj_z, proj_qvg, proj_qvg, proj_z, proj_qvg)


def _route_rows(logits_t, utri, cnt_ref):
    tm = logits_t.shape[1]
    gi = lax.broadcasted_iota(jnp.int32, (8, tm), 0).astype(F32)
    lg = jnp.where(gi < N_GROUPS, logits_t[0:8], NEG_INF)
    gm = jnp.max(lg, axis=0, keepdims=True)
    p_grp = 1.0 / jnp.sum(jnp.exp(lg - gm), axis=0, keepdims=True)
    grp = jnp.min(jnp.where(lg == gm, gi, 99.0), axis=0, keepdims=True)
    ei = lax.broadcasted_iota(jnp.int32, (N_EXPERTS, tm), 0).astype(F32)
    egrp = (lax.broadcasted_iota(jnp.int32, (N_EXPERTS, tm), 0) >> 3).astype(F32)
    le = jnp.where(egrp == grp, logits_t[8:8 + N_EXPERTS], NEG_INF)
    v1 = jnp.max(le, axis=0, keepdims=True)
    i1 = jnp.min(jnp.where(le == v1, ei, 99.0), axis=0, keepdims=True)
    le2 = jnp.where(ei == i1, NEG_INF, le)
    v2 = jnp.max(le2, axis=0, keepdims=True)
    i2 = jnp.min(jnp.where(le2 == v2, ei, 99.0), axis=0, keepdims=True)
    t = jnp.exp(v2 - v1)
    g1 = p_grp / (1.0 + t)
    g2 = p_grp * t / (1.0 + t)
    o1 = jnp.where(ei == i1, 1.0, 0.0)
    o2 = jnp.where(ei == i2, 1.0, 0.0)
    osum = o1 + o2
    before = _dot(osum.astype(BF16), utri) + cnt_ref[:, 0:1]
    rank1 = jnp.sum(o1 * before, axis=0, keepdims=True)
    rank2 = jnp.sum(o2 * before, axis=0, keepdims=True)
    cnt_ref[...] = cnt_ref[...] + jnp.sum(osum, axis=1, keepdims=True)

    def pieces(g):
        hi = g.astype(BF16).astype(F32)
        r = g - hi
        mid = r.astype(BF16).astype(F32)
        return [hi, mid, r - mid]

    rows = [i1, i2, rank1, rank2] + pieces(g1) + pieces(g2)
    return rows + [jnp.zeros((1, tm), F32)] * (ROUTE_ROWS - len(rows))


def _mix_out_kernel(*refs, rec, alpha, n_x, nb0):
    x_refs, refs = refs[:n_x], refs[n_x:]
    if rec:
        (a_ref, b_ref, g_ref, gn_ref, w_ref, mod_ref, lng_ref, lnb_ref,
         rw_hi_ref, rw_lo_ref, rb_ref, utri_ref, xo_ref, route_ref, cnt_out_ref, cnt_ref) = refs
    else:
        (a_ref, b_ref, w_ref, mod_ref, lng_ref, lnb_ref,
         rw_hi_ref, rw_lo_ref, rb_ref, utri_ref, xo_ref, route_ref, cnt_out_ref, cnt_ref) = refs
    i = pl.program_id(0)

    @pl.when(i == 0)
    def _():
        cnt_ref[...] = jnp.zeros_like(cnt_ref)

    if rec:
        o = a_ref[...] + b_ref[...]
        g = g_ref[...].astype(F32)
        gate = gn_ref[...] * (g * jax.nn.sigmoid(g))
        parts = []
        for h in range(C_HEADS):
            blk = o[:, h * C_KEY_DIM:(h + 1) * C_KEY_DIM]
            ms = jnp.mean(blk * blk, axis=-1, keepdims=True)
            parts.append(blk * lax.rsqrt(ms + RMS_EPS))
        lhs = (jnp.concatenate(parts, axis=1) * gate).astype(BF16)
        y = _dot(lhs, w_ref[...])
    else:
        half = a_ref.shape[1]
        y = _dot(a_ref[...], w_ref[0:half, :]) + _dot(b_ref[...], w_ref[half:, :])
    m = mod_ref[0]
    xn = _layer_norm(alpha * _x_block(x_refs, nb0) + (1.0 + m[2:3]) * y, lng_ref[...], lnb_ref[...])
    xo_ref[...] = xn
    h_hi, h_lo = _split2(xn * (1.0 + m[4:5]) + m[3:4])
    rw_hi = rw_hi_ref[...]
    logits_t = _dot_nt(rw_hi, h_hi) + _dot_nt(rw_hi, h_lo) + _dot_nt(rw_lo_ref[...], h_hi) + rb_ref[:, 0:1]
    for k, row in enumerate(_route_rows(logits_t, utri_ref[...], cnt_ref)):
        route_ref[k:k + 1, :] = row
    cnt_out_ref[...] = cnt_ref[...]


def _router_tables(w_rg, b_rg, w_re, b_re):
    d = w_rg.shape[0]
    wt = jnp.zeros((ROUTER_ROWS, d), F32)
    wt = wt.at[0:N_GROUPS].set(w_rg.astype(F32).T).at[8:8 + N_EXPERTS].set(w_re.astype(F32).T)
    hi = wt.astype(BF16)
    lo = (wt - hi.astype(F32)).astype(BF16)
    rb = jnp.zeros((ROUTER_ROWS,), F32).at[0:N_GROUPS].set(b_rg.astype(F32))
    rb = rb.at[8:8 + N_EXPERTS].set(b_re.astype(F32).reshape(-1))
    return hi, lo, jnp.broadcast_to(rb[:, None], (ROUTER_ROWS, LANES))


def _mix_out(lhs, w_out, xs, mod_l, ln_g, ln_b, router, segs, *, alpha, rec_extra=None):
    nt = sum(x.shape[0] for x in xs)
    d = xs[0].shape[1]
    rec = rec_extra is not None
    rw_hi, rw_lo, rb = router
    idx = np.arange(TM)
    utri = jnp.asarray(idx[:, None] < idx[None, :], BF16)
    row = lambda i: (i, 0)
    const = lambda i: (0, 0)
    a, b = lhs
    in_specs = _x_specs(xs, TM) + [pl.BlockSpec((TM, a.shape[1]), row), pl.BlockSpec((TM, b.shape[1]), row)]
    args = [*xs, a, b]
    if rec:
        proj, gnorm = rec_extra
        in_specs += [pl.BlockSpec((TM, d), lambda i: (i, 2)), pl.BlockSpec((1, d), const)]
        args += [proj, gnorm.reshape(1, d)]
    in_specs += [pl.BlockSpec(w_out.shape, const),
                 pl.BlockSpec((1, 6, d), lambda i: (_mod_row(i, TM, segs), 0, 0)),
                 pl.BlockSpec((1, d), const), pl.BlockSpec((1, d), const),
                 pl.BlockSpec(rw_hi.shape, const), pl.BlockSpec(rw_lo.shape, const),
                 pl.BlockSpec(rb.shape, const), pl.BlockSpec((TM, TM), const)]
    args += [w_out, mod_l, ln_g.reshape(1, d), ln_b.reshape(1, d), rw_hi, rw_lo, rb, utri]
    return pl.pallas_call(
        functools.partial(_mix_out_kernel, rec=rec, alpha=alpha, n_x=len(xs), nb0=xs[0].shape[0] // TM),
        grid=(nt // TM,),
        in_specs=in_specs,
        out_specs=[pl.BlockSpec((TM, d), row), pl.BlockSpec((ROUTE_ROWS, TM), lambda i: (0, i)),
                   pl.BlockSpec((N_EXPERTS, LANES), const)],
        out_shape=[jax.ShapeDtypeStruct((nt, d), F32), jax.ShapeDtypeStruct((ROUTE_ROWS, nt), F32),
                   jax.ShapeDtypeStruct((N_EXPERTS, LANES), F32)],
        scratch_shapes=[pltpu.VMEM((N_EXPERTS, LANES), F32)],
        compiler_params=_cparams(("arbitrary",)),
        name="mix_out_rec" if rec else "mix_out_attn",
    )(*args)


def _rows_to_tiles(ref, x):
    n = x.shape[0]
    for s in range(ROW_TILE):
        ref[pl.ds(s, n, stride=ROW_TILE), :] = x[:, s * LANES:(s + 1) * LANES]


def _tiles_to_rows(ref, n):
    return jnp.concatenate([ref[pl.ds(s, n, stride=ROW_TILE), :] for s in range(ROW_TILE)], axis=1)


def _dest_kernel(ps_ref, route_ref, o0_ref, o1_ref):
    r = route_ref[...]
    for k, o_ref in enumerate((o0_ref, o1_ref)):
        eid = r[k:k + 1].astype(jnp.int32)
        acc = jnp.zeros_like(eid)
        for e in range(N_EXPERTS):
            acc = jnp.where(eid == e, ps_ref[e], acc)
        o_ref[...] = (acc + r[2 + k:3 + k].astype(jnp.int32)) * ROW_TILE


def _dest(route, pad_starts):
    nt = route.shape[1]
    tb = 2048
    return pl.pallas_call(
        _dest_kernel,
        grid_spec=pltpu.PrefetchScalarGridSpec(
            num_scalar_prefetch=1,
            grid=(nt // tb,),
            in_specs=[pl.BlockSpec((ROUTE_ROWS, tb), lambda i, ps: (0, i))],
            out_specs=[pl.BlockSpec((1, tb), lambda i, ps: (0, i))] * 2,
        ),
        out_shape=[jax.ShapeDtypeStruct((1, nt), jnp.int32)] * 2,
        compiler_params=_cparams(("parallel",)),
        name="moe_dest",
    )(pad_starts, route)


def _dispatch_kernel(d0_ref, d1_ref, x_ref, mod_ref, xbuf_in_ref, xbuf_ref, h_ref, sem):
    del xbuf_in_ref
    i = pl.program_id(0)
    slot = lax.rem(i, 2)
    hs = h_ref.at[slot]
    m = mod_ref[0]
    _rows_to_tiles(hs, x_ref[...] * (1.0 + m[4:5]) + m[3:4])

    def body(t, carry):
        src = hs.at[pl.ds(pl.multiple_of(t * ROW_TILE, ROW_TILE), ROW_TILE)]
        for k, d_ref in enumerate((d0_ref, d1_ref)):
            dst = xbuf_ref.at[pl.ds(pl.multiple_of(d_ref[0, t], ROW_TILE), ROW_TILE)]
            pltpu.make_async_copy(src, dst, sem.at[slot, k]).start(priority=k)
        return carry

    lax.fori_loop(0, TM, body, 0, unroll=8)

    def drain(s):
        for k in range(2):
            pltpu.make_async_copy(h_ref.at[s], xbuf_ref.at[pl.ds(0, TM * ROW_TILE)], sem.at[s, k]).wait()

    @pl.when(i > 0)
    def _():
        drain(1 - slot)

    @pl.when(i == pl.num_programs(0) - 1)
    def _():
        drain(slot)


def _dispatch(x, mod_l, dest, xbuf, segs):
    nt, d = x.shape
    smem_row = pl.BlockSpec((1, TM), lambda i: (0, i), memory_space=pltpu.SMEM)
    return pl.pallas_call(
        _dispatch_kernel,
        grid=(nt // TM,),
        in_specs=[smem_row, smem_row,
                  pl.BlockSpec((TM, d), lambda i: (i, 0)),
                  pl.BlockSpec((1, 6, d), lambda i: (_mod_row(i, TM, segs), 0, 0)),
                  pl.BlockSpec(memory_space=pl.ANY)],
        out_specs=pl.BlockSpec(memory_space=pl.ANY),
        out_shape=jax.ShapeDtypeStruct(xbuf.shape, F32),
        scratch_shapes=[pltpu.VMEM((2, TM * ROW_TILE, LANES), F32), pltpu.SemaphoreType.DMA((2, 2))],
        input_output_aliases={4: 0},
        compiler_params=_cparams(("arbitrary",)),
        name="moe_dispatch",
    )(dest[0], dest[1], x, mod_l, xbuf)


def _expert_kernel(be_ref, nu_ref, x_ref, wg_ref, wu_ref, wd_ref, o_ref, wgb_ref, wub_ref, wdb_ref):
    i = pl.program_id(0)
    cur = jnp.minimum(i, nu_ref[0] - 1)
    new_expert = jnp.logical_or(i == 0, be_ref[cur] != be_ref[jnp.maximum(cur - 1, 0)])

    @pl.when(new_expert)
    def _():
        wgb_ref[...] = wg_ref[0, 0].astype(BF16)
        wub_ref[...] = wu_ref[0, 0].astype(BF16)
        wdb_ref[...] = wd_ref[0, 0].astype(BF16)

    @pl.when(i < nu_ref[0])
    def _():
        xb = _tiles_to_rows(x_ref, MOE_BLK).astype(BF16)
        hg = _dot(xb, wgb_ref[...])
        hu = _dot(xb, wub_ref[...])
        hid = (hg * jax.nn.sigmoid(hg) * hu).astype(BF16)
        _rows_to_tiles(o_ref, _dot(hid, wdb_ref[...]))

    @pl.when(i >= nu_ref[0])
    def _():
        o_ref[...] = jnp.zeros_like(o_ref)


def _experts(xbuf, blk_exp, n_used, wg, wu, wd, layer):
    n_blk = xbuf.shape[0] // (MOE_BLK * ROW_TILE)
    blk = lambda i, be, nu: (jnp.minimum(i, nu[0] - 1), 0)
    oblk = lambda i, be, nu: (i, 0)
    wmap = lambda i, be, nu: (layer, be[jnp.minimum(i, nu[0] - 1)], 0, 0)
    return pl.pallas_call(
        _expert_kernel,
        grid_spec=pltpu.PrefetchScalarGridSpec(
            num_scalar_prefetch=2,
            grid=(n_blk,),
            in_specs=[pl.BlockSpec((MOE_BLK * ROW_TILE, LANES), blk),
                      pl.BlockSpec((1, 1, D_MODEL, D_EXPERT), wmap),
                      pl.BlockSpec((1, 1, D_MODEL, D_EXPERT), wmap),
                      pl.BlockSpec((1, 1, D_EXPERT, D_MODEL), wmap)],
            out_specs=pl.BlockSpec((MOE_BLK * ROW_TILE, LANES), oblk),
            scratch_shapes=[pltpu.VMEM((D_MODEL, D_EXPERT), BF16), pltpu.VMEM((D_MODEL, D_EXPERT), BF16),
                            pltpu.VMEM((D_EXPERT, D_MODEL), BF16)],
        ),
        out_shape=jax.ShapeDtypeStruct(xbuf.shape, F32),
        compiler_params=_cparams(("arbitrary",)),
        name="moe_experts",
    )(blk_exp, n_used, xbuf, wg, wu, wd)


def _combine_kernel(d0_ref, d1_ref, n0_ref, n1_ref, ybuf_ref, x_ref, mod_ref, route_ref, sel_ref, lng_ref, lnb_ref,
                    *rest, alpha, split_blk):
    o_refs, (y_ref, sem) = rest[:-2], rest[-2:]
    i = pl.program_id(0)
    slot = lax.rem(i, 2)

    def gather(dk_refs, s):
        def body(t, carry):
            row = pl.ds(pl.multiple_of(t * ROW_TILE, ROW_TILE), ROW_TILE)
            for k, d_ref in enumerate(dk_refs):
                src = ybuf_ref.at[pl.ds(pl.multiple_of(d_ref[0, t], ROW_TILE), ROW_TILE)]
                pltpu.make_async_copy(src, y_ref.at[s, k, row], sem.at[s, k]).start(priority=k)
            return carry

        lax.fori_loop(0, TM, body, 0, unroll=8)

    @pl.when(i == 0)
    def _():
        gather((d0_ref, d1_ref), slot)

    @pl.when(i + 1 < pl.num_programs(0))
    def _():
        gather((n0_ref, n1_ref), 1 - slot)

    gates = _dot_tn(route_ref[...].astype(BF16), sel_ref[...])
    for k in range(2):
        pltpu.make_async_copy(ybuf_ref.at[pl.ds(0, TM * ROW_TILE)], y_ref.at[slot, k], sem.at[slot, k]).wait()
    m = mod_ref[0]
    g0 = jnp.concatenate([gates[:, 0:LANES]] * ROW_TILE, axis=1)
    g1 = jnp.concatenate([gates[:, LANES:]] * ROW_TILE, axis=1)
    y = _tiles_to_rows(y_ref.at[slot, 0], TM) * g0 + _tiles_to_rows(y_ref.at[slot, 1], TM) * g1
    out = _layer_norm(alpha * x_ref[...] + (1.0 + m[5:6]) * y, lng_ref[...], lnb_ref[...])
    if split_blk is None:
        o_refs[0][...] = out
    else:
        @pl.when(pl.program_id(0) < split_blk)
        def _():
            o_refs[0][...] = out

        @pl.when(pl.program_id(0) >= split_blk)
        def _():
            o_refs[1][...] = out


def _gate_select():
    sel = np.zeros((ROUTE_ROWS, 2 * LANES), np.float32)
    sel[4:7, 0:LANES] = 1.0
    sel[7:10, LANES:] = 1.0
    return jnp.asarray(sel, BF16)


def _combine(ybuf, dest, route, x, mod_l, ln_g, ln_b, segs, *, alpha, split=False):
    nt, d = x.shape
    const = lambda i: (0, 0)
    last = nt // TM - 1
    smem_row = pl.BlockSpec((1, TM), lambda i: (0, i), memory_space=pltpu.SMEM)
    smem_next = pl.BlockSpec((1, TM), lambda i: (0, jnp.minimum(i + 1, last)), memory_space=pltpu.SMEM)
    if split:
        n0 = segs[0][0] * segs[0][1]
        split_blk = n0 // TM
        out_specs = [pl.BlockSpec((TM, d), lambda i: (jnp.minimum(i, split_blk - 1), 0)),
                     pl.BlockSpec((TM, d), lambda i: (jnp.maximum(i - split_blk, 0), 0))]
        out_shape = [jax.ShapeDtypeStruct((n0, d), F32), jax.ShapeDtypeStruct((nt - n0, d), F32)]
    else:
        split_blk = None
        out_specs = [pl.BlockSpec((TM, d), lambda i: (i, 0))]
        out_shape = [jax.ShapeDtypeStruct((nt, d), F32)]
    res = pl.pallas_call(
        functools.partial(_combine_kernel, alpha=alpha, split_blk=split_blk),
        grid=(nt // TM,),
        in_specs=[smem_row, smem_row, smem_next, smem_next,
                  pl.BlockSpec(memory_space=pl.ANY),
                  pl.BlockSpec((TM, d), lambda i: (i, 0)),
                  pl.BlockSpec((1, 6, d), lambda i: (_mod_row(i, TM, segs), 0, 0)),
                  pl.BlockSpec((ROUTE_ROWS, TM), lambda i: (0, i)),
                  pl.BlockSpec((ROUTE_ROWS, 2 * LANES), const),
                  pl.BlockSpec((1, d), const), pl.BlockSpec((1, d), const)],
        out_specs=out_specs,
        out_shape=out_shape,
        scratch_shapes=[pltpu.VMEM((2, 2, TM * ROW_TILE, LANES), F32), pltpu.SemaphoreType.DMA((2, 2))],
        compiler_params=_cparams(("arbitrary",)),
        name="moe_combine",
    )(dest[0], dest[1], dest[0], dest[1], ybuf, x, mod_l, route, _gate_select(), ln_g.reshape(1, d),
      ln_b.reshape(1, d))
    return res if split else res[0]


def _moe_slots(nt):
    return 2 * nt + N_EXPERTS * MOE_BLK


def _moe(x, mod_l, route, counts, ln_g, ln_b, wg, wu, wd, layer, xbuf, segs, *, alpha, split=False):
    n_blk = _moe_slots(x.shape[0]) // MOE_BLK
    cnt = counts[:, 0].astype(jnp.int32)
    padded = (cnt + MOE_BLK - 1) // MOE_BLK * MOE_BLK
    pad_ends = jnp.cumsum(padded)
    pad_starts = (pad_ends - padded).astype(jnp.int32)
    blk_start = jnp.arange(n_blk, dtype=jnp.int32) * MOE_BLK
    blk_exp = jnp.minimum(jnp.sum(pad_ends[None, :] <= blk_start[:, None], axis=1), N_EXPERTS - 1).astype(jnp.int32)
    n_used = (pad_ends[-1:] // MOE_BLK).astype(jnp.int32)
    dest = _dest(route, pad_starts)
    xbuf = _dispatch(x, mod_l, dest, xbuf, segs)
    ybuf = _experts(xbuf, blk_exp, n_used, wg, wu, wd, layer)
    return _combine(ybuf, dest, route, x, mod_l, ln_g, ln_b, segs, alpha=alpha, split=split), xbuf


def _attn_w_in_layout(w_in):
    scale = HEAD_DIM ** -0.5
    qa = w_in[:, 0:512] * scale
    ka = w_in[:, 512:640]
    va = w_in[:, 640:768]
    qb = w_in[:, 768:1280] * scale
    kb = w_in[:, 1280:1792]
    vb = w_in[:, 1792:2304]
    dup = lambda a: jnp.concatenate([a[:, 0:64], a[:, 0:64], a[:, 64:128], a[:, 64:128]], axis=1)
    return jnp.concatenate([qa, dup(ka), dup(va), qb, kb, vb], axis=1).astype(BF16)


def _rec_w_in_layout(w_in):
    d = D_MODEL
    return jnp.concatenate([w_in[:, d:3 * d], w_in[:, 0:d], w_in[:, 3 * d:5 * d]], axis=1).astype(BF16)


def _rope_tables(t_max):
    inv = jnp.power(ROPE_THETA, -jnp.arange(0, HEAD_DIM, 2, dtype=F32) / HEAD_DIM)
    ang = jnp.arange(t_max, dtype=F32)[:, None] * inv[None, :]
    cos = jnp.cos(ang)
    sin = jnp.sin(ang)
    cos128 = jnp.concatenate([cos, cos, cos, cos], axis=1)
    sin128 = jnp.concatenate([-sin, sin, -sin, sin], axis=1)
    return cos128, sin128


def kernel(x_prompt, x_sample, c_prompt, c_sample, ada_w, ada_b, ln_g, ln_b, attn_w_in, attn_sink, nat_rel_bias, attn_w_out, rec_w_in, rec_lb, rec_gnorm, rec_w_out, router_w_group, router_b_group, router_w_expert, router_b_expert, expert_w_gate, expert_w_up, expert_w_down):
    depth = ada_w.shape[0]
    d = x_prompt.shape[-1]
    segs = ((x_prompt.shape[0], x_prompt.shape[1]), (x_sample.shape[0], x_sample.shape[1]))
    alpha = (2 * depth) ** 0.25
    xs = (x_prompt.reshape(-1, d), x_sample.reshape(-1, d))
    nt = xs[0].shape[0] + xs[1].shape[0]
    c = jnp.concatenate([c_prompt, c_sample], axis=0)
    nseq = c.shape[0]
    mod = _adaln(c, ada_w, ada_b).reshape(depth, nseq, 6, d)
    rope_tables = _rope_tables(max(t for _, t in segs))
    xbuf = jnp.zeros((_moe_slots(nt) * ROW_TILE, LANES), F32)
    for l in range(depth):
        mod_l = mod[l]
        i = l // 2
        router = _router_tables(router_w_group[l], router_b_group[l], router_w_expert[l], router_b_expert[l])
        if l % 2 == 0:
            proj = _inproj_attn(xs, mod_l, _attn_w_in_layout(attn_w_in[i]), segs, rope_tables)
            oa = _win_attn(proj, attn_sink[i].astype(F32), segs)
            ob = _nat_attn(proj, _nat_bias_table(nat_rel_bias[i]), segs)
            x, route, counts = _mix_out((oa, ob), attn_w_out[i].astype(BF16), xs, mod_l, ln_g[l, 0], ln_b[l, 0],
                                        router, segs, alpha=alpha)
        else:
            proj_z, proj_qvg = _inproj_rec(xs[0], mod_l, _rec_w_in_layout(rec_w_in[i]), segs)
            o_f, o_b = _hgrn(proj_z, proj_qvg, rec_lb.astype(F32), i, segs)
            x, route, counts = _mix_out((o_f, o_b), rec_w_out[i].astype(BF16), xs, mod_l, ln_g[l, 0], ln_b[l, 0],
                                        router, segs, alpha=alpha, rec_extra=(proj_qvg, rec_gnorm[i].astype(F32)))
        x, xbuf = _moe(x, mod_l, route, counts, ln_g[l, 1], ln_b[l, 1], expert_w_gate, expert_w_up, expert_w_down,
                       l, xbuf, segs, alpha=alpha, split=(l == depth - 1))
        xs = (x,) if l < depth - 1 else x
    return (xs[0].reshape(x_prompt.shape), xs[1].reshape(x_sample.shape))
```
